```python
import math
import jax
import jax.numpy as jnp
from jax import lax
import numpy as np

D_MODEL = 2048
BATCH = 16
SEQ = 256
DEPTH = 4
DEC_BATCH = 8
DEC_SEQ = 4096
PAST_LEN = 256

F32 = jnp.float32
GRID_W = 64
POS_BASE = 10000.0
N_EVEN = (DEPTH + 1) // 2
N_ODD = DEPTH // 2
MIX_W = D_MODEL
D_FF = 5632
N_MOD = 9
EPS = 1e-6
FNET_W = MIX_W // 2
FNET_GROUPS = 4
FNET_GC = FNET_W // FNET_GROUPS
HY_W = MIX_W // 2
HY_BANDS = 16
HY_EMB = 2 * HY_BANDS + 1
HY_FILT_H = 64
HY_DECAY_LO = -math.log(1e-2) / 1.5
HY_DECAY_HI = -math.log(1e-2) / 0.3
RET_W = MIX_W // 2
RET_HEADS = 4
RET_DK = RET_W // RET_HEADS
RET_DV = RET_W // RET_HEADS
RET_CHUNK = 128
POOL_W = MIX_W - RET_W
POOL_WINDOWS = (2, 4, 8, 16)
POOL_GC = POOL_W // len(POOL_WINDOWS)
EVEN_IN = FNET_W + 3 * HY_W
ODD_IN = 4 * RET_W + POOL_W

kernel_name = 'hybrid_fnet_hyena_retention_pool_dit_step'


def rmsnorm(x, g):
    xf = x.astype(F32)
    y = xf * lax.rsqrt(jnp.mean(xf * xf, axis=-1, keepdims=True) + EPS)
    return (y * g.astype(F32)).astype(x.dtype)


def modulate(h, shift, scale):
    return h * (1.0 + scale[..., None, :]) + shift[..., None, :]


def swiglu(h, w_in, w_out):
    gate, up = jnp.split(h @ w_in, 2, axis=-1)
    return (jax.nn.silu(gate) * up) @ w_out


def grid_pos_embed(n_tokens, dtype):
    rows = n_tokens // GRID_W
    rr, cc = jnp.meshgrid(jnp.arange(rows, dtype=F32), jnp.arange(GRID_W, dtype=F32), indexing='ij')
    rr = rr.reshape(-1)[:, None]
    cc = cc.reshape(-1)[:, None]
    quarter = D_MODEL // 4
    omega = 1.0 / (POS_BASE ** (jnp.arange(quarter, dtype=F32) / quarter))
    ar, ac = rr * omega, cc * omega
    return jnp.concatenate([jnp.sin(ar), jnp.cos(ar), jnp.sin(ac), jnp.cos(ac)], axis=-1).astype(dtype)


def fourier_mix(u, w):
    b, n, _ = u.shape
    ug = u.astype(F32).reshape(b, n, FNET_GROUPS, FNET_GC)
    f = jnp.fft.fft2(ug, axes=(1, 3), norm='ortho').real
    y = jnp.einsum('bngc,gcd->bngd', f, w.astype(F32))
    return y.reshape(b, n, FNET_W).astype(u.dtype)


def short_conv3(u, w, bias):
    n = u.shape[1]
    up = jnp.pad(u, ((0, 0), (1, 1), (0, 0)))
    return up[:, :n] * w[0] + up[:, 1:n + 1] * w[1] + up[:, 2:] * w[2] + bias


def hyena_filter(n, w1, b1, w2, b2, w_out, freq, decay):
    t_idx = jnp.arange(n, dtype=F32)
    t = jnp.linspace(0.0, 1.0, n, dtype=F32)
    bands = jnp.linspace(1e-4, HY_BANDS - 1, HY_BANDS, dtype=F32)
    ang = (2.0 * math.pi / n) * t_idx[:, None] * bands[None, :]
    z = jnp.concatenate([t[:, None], jnp.cos(ang), jnp.sin(ang)], axis=-1)
    fr = freq.astype(F32)
    h = jnp.sin(fr * (z @ w1.astype(F32) + b1.astype(F32)))
    h = jnp.sin(fr * (h @ w2.astype(F32) + b2.astype(F32)))
    h = h @ w_out.astype(F32)
    window = jnp.exp(-t[:, None] * jnp.abs(decay.astype(F32))[None, :])
    h_f = h[:, :HY_W] * window
    h_b = h[:, HY_W:] * window
    taps = jnp.concatenate([h_f, jnp.zeros((1, HY_W), F32), h_b[:0:-1]], axis=0)
    return taps / jnp.sum(jnp.abs(taps), axis=0, keepdims=True)


def hyena_mix(u, conv_w, conv_b, w1, b1, w2, b2, w_out, freq, decay, bias):
    b, n, _ = u.shape
    uc = short_conv3(u, conv_w, conv_b).astype(F32)
    x0, x1, v = jnp.split(uc, 3, axis=-1)
    taps = hyena_filter(n, w1, b1, w2, b2, w_out, freq, decay)
    v = v * x1
    vf = jnp.fft.rfft(v, n=2 * n, axis=1)
    tf = jnp.fft.rfft(taps, axis=0)
    conv = jnp.fft.irfft(vf * tf[None], n=2 * n, axis=1)[:, :n]
    y = x0 * (conv + v * bias.astype(F32))
    return y.astype(u.dtype)


def pool_mix(u, w, scale):
    b, n, _ = u.shape
    ug = u.astype(F32).reshape(b, n, len(POOL_WINDOWS), POOL_GC)
    cs = jnp.concatenate([jnp.zeros((b, 1) + ug.shape[2:], F32), jnp.cumsum(ug, axis=1)], axis=1)
    t = jnp.arange(n)
    outs = []
    for g, win in enumerate(POOL_WINDOWS):
        lo = jnp.clip(t - win // 2, 0, n - 1)
        hi = jnp.clip(t - win // 2 + win - 1, 0, n - 1)
        cs_g = cs[:, :, g]
        total = cs_g[:, hi + 1] - cs_g[:, lo]
        count = (hi - lo + 1).astype(F32)[None, :, None]
        outs.append(total / count - ug[:, :, g])
    p = jnp.stack(outs, axis=2)
    y = jnp.einsum('bngc,gcd->bngd', p, w.astype(F32)).reshape(b, n, POOL_W) * scale.astype(F32)
    return y.astype(u.dtype)


def retention_scan(q, k, v, log_gamma, s0):
    b, n, h, _ = q.shape
    n_chunks = n // RET_CHUNK

    def chunks(a):
        return a.reshape(b, n_chunks, RET_CHUNK, h, a.shape[-1]).transpose(1, 0, 3, 2, 4)

    idx = jnp.arange(RET_CHUNK, dtype=F32)
    lg = log_gamma.astype(F32)[:, None]
    diff = idx[:, None] - idx[None, :]
    inner_decay = jnp.where(diff[None] >= 0, jnp.exp(lg[:, :, None] * jnp.maximum(diff, 0.0)[None]), 0.0)
    q_decay = jnp.exp(lg * (idx + 1.0))
    k_decay = jnp.exp(lg * (RET_CHUNK - 1.0 - idx))
    chunk_decay = jnp.exp(lg[:, 0] * RET_CHUNK)

    def step(s, qkv):
        qc, kc, vc = qkv
        scores = jnp.einsum('bhid,bhjd->bhij', qc, kc) * inner_decay
        inner = jnp.einsum('bhij,bhjv->bhiv', scores, vc)
        cross = jnp.einsum('bhid,bhdv->bhiv', qc, s) * q_decay[None, :, :, None]
        s_new = s * chunk_decay[None, :, None, None] + jnp.einsum(
            'bhjd,bhjv->bhdv', kc * k_decay[None, :, :, None], vc)
        return s_new, inner + cross

    s_fin, out = lax.scan(step, s0, (chunks(q), chunks(k), chunks(v)))
    out = out.transpose(1, 0, 3, 2, 4).reshape(b, n, h, -1)
    return out, s_fin


def retention_mix(u, log_g_f, log_g_b, gn, s0_f, s0_b):
    b, n, _ = u.shape
    q, k, v, g = jnp.split(u, 4, axis=-1)

    def heads(a):
        return a.astype(F32).reshape(b, n, RET_HEADS, -1)

    q, k, v = heads(q), heads(k) * (RET_DK ** -0.5), heads(v)
    o_f, s_f = retention_scan(q, k, v, log_g_f, s0_f.astype(F32))
    o_b, s_b = retention_scan(q[:, ::-1], k[:, ::-1], v[:, ::-1], log_g_b, s0_b.astype(F32))
    o = o_f + o_b[:, ::-1]
    mu = jnp.mean(o, axis=-1, keepdims=True)
    var = jnp.mean(jnp.square(o - mu), axis=-1, keepdims=True)
    o = ((o - mu) * lax.rsqrt(var + EPS)).reshape(b, n, RET_W) * gn.astype(F32)
    y = jax.nn.silu(g.astype(F32)) * o
    return y.astype(u.dtype), s_f, s_b


def setup_inputs(seed: int = 0) -> dict:
    key = jax.random.key(seed)
    k = jax.random.split(key, 34)

    def nrm(kk, shape, s):
        return jax.random.normal(kk, shape, F32) * s

    ret_base = jnp.log1p(-(2.0 ** (-jnp.linspace(5.0, 12.0, RET_HEADS, dtype=F32))))
    hy_base = jnp.linspace(HY_DECAY_LO, HY_DECAY_HI, HY_W, dtype=F32)
    return {
        'x_prompt': nrm(k[0], (BATCH, SEQ, D_MODEL), 1.0),
        'x_sample': nrm(k[1], (DEC_BATCH, DEC_SEQ, D_MODEL), 1.0),
        'state_ret_fwd': nrm(k[2], (DEC_BATCH, N_ODD, RET_HEADS, RET_DK, RET_DV), 1.0),
        'state_ret_bwd': nrm(k[3], (DEC_BATCH, N_ODD, RET_HEADS, RET_DK, RET_DV), 1.0),
        'c': nrm(k[4], (DEC_BATCH, D_MODEL), 1.0),
        'c_ctx': nrm(k[5], (D_MODEL,), 1.0),
        'norm_g': 1.0 + nrm(k[6], (DEPTH, 3, D_MODEL), 0.02),
        'mod_w': nrm(k[7], (DEPTH, D_MODEL, N_MOD * D_MODEL), 0.5 * D_MODEL ** -0.5),
        'mod_b': nrm(k[8], (DEPTH, N_MOD * D_MODEL), 0.02),
        'ffn_a_in': nrm(k[9], (DEPTH, D_MODEL, 2 * D_FF), D_MODEL ** -0.5),
        'ffn_a_out': nrm(k[10], (DEPTH, D_FF, D_MODEL), D_FF ** -0.5),
        'ffn_b_in': nrm(k[11], (DEPTH, D_MODEL, 2 * D_FF), D_MODEL ** -0.5),
        'ffn_b_out': nrm(k[12], (DEPTH, D_FF, D_MODEL), D_FF ** -0.5),
        'ev_in_w': nrm(k[13], (N_EVEN, D_MODEL, EVEN_IN), D_MODEL ** -0.5),
        'ev_out_w': nrm(k[14], (N_EVEN, MIX_W, D_MODEL), MIX_W ** -0.5),
        'fnet_w': nrm(k[15], (N_EVEN, FNET_GROUPS, FNET_GC, FNET_GC), FNET_GC ** -0.5),
        'hy_conv_w': nrm(k[16], (N_EVEN, 3, 3 * HY_W), 3.0 ** -0.5),
        'hy_conv_b': nrm(k[17], (N_EVEN, 3 * HY_W), 0.02),
        'hy_w1': nrm(k[18], (N_EVEN, HY_EMB, HY_FILT_H), HY_EMB ** -0.5),
        'hy_b1': nrm(k[19], (N_EVEN, HY_FILT_H), 0.1),
        'hy_w2': nrm(k[20], (N_EVEN, HY_FILT_H, HY_FILT_H), HY_FILT_H ** -0.5),
        'hy_b2': nrm(k[21], (N_EVEN, HY_FILT_H), 0.1),
        'hy_w_out': nrm(k[22], (N_EVEN, HY_FILT_H, 2 * HY_W), HY_FILT_H ** -0.5),
        'hy_freq': 1.0 + nrm(k[23], (N_EVEN, HY_FILT_H), 0.1),
        'hy_decay': hy_base * (1.0 + nrm(k[24], (N_EVEN, HY_W), 0.01)),
        'hy_bias': nrm(k[25], (N_EVEN, HY_W), 0.1),
        'od_in_w': nrm(k[26], (N_ODD, D_MODEL, ODD_IN), D_MODEL ** -0.5),
        'od_out_w': nrm(k[27], (N_ODD, MIX_W, D_MODEL), MIX_W ** -0.5),
        'ret_log_decay_fwd': ret_base * (1.0 + nrm(k[28], (N_ODD, RET_HEADS), 0.01)),
        'ret_log_decay_bwd': ret_base * (1.0 + nrm(k[29], (N_ODD, RET_HEADS), 0.01)),
        'ret_gn': 1.0 + nrm(k[30], (N_ODD, RET_W), 0.02),
        'pool_w': nrm(k[31], (N_ODD, len(POOL_WINDOWS), POOL_GC, POOL_GC), POOL_GC ** -0.5),
        'pool_scale': 1.0 + nrm(k[32], (N_ODD, POOL_W), 0.02),
        'final_norm': 1.0 + nrm(k[33], (D_MODEL,), 0.02),
    }


def reference(x_prompt, x_sample, state_ret_fwd, state_ret_bwd, c, c_ctx, norm_g, mod_w, mod_b,
              ffn_a_in, ffn_a_out, ffn_b_in, ffn_b_out, ev_in_w, ev_out_w, fnet_w, hy_conv_w, hy_conv_b,
              hy_w1, hy_b1, hy_w2, hy_b2, hy_w_out, hy_freq, hy_decay, hy_bias, od_in_w, od_out_w,
              ret_log_decay_fwd, ret_log_decay_bwd, ret_gn, pool_w, pool_scale, final_norm):

    def trunk(x, cond, s0_f, s0_b):
        sc = jax.nn.silu(cond)
        s_f_out, s_b_out = [], []
        for l in range(DEPTH):
            mod = sc @ mod_w[l] + mod_b[l]
            sh1, sc1, g1, sh2, sc2, g2, sh3, sc3, g3 = jnp.split(mod, N_MOD, axis=-1)
            h = modulate(rmsnorm(x, norm_g[l, 0]), sh1, sc1)
            x = x + 0.5 * g1[..., None, :] * swiglu(h, ffn_a_in[l], ffn_a_out[l])
            h = modulate(rmsnorm(x, norm_g[l, 1]), sh2, sc2)
            if l % 2 == 0:
                e = l // 2
                u = h @ ev_in_w[e]
                ya = fourier_mix(u[..., :FNET_W], fnet_w[e])
                yb = hyena_mix(u[..., FNET_W:], hy_conv_w[e], hy_conv_b[e], hy_w1[e], hy_b1[e], hy_w2[e],
                               hy_b2[e], hy_w_out[e], hy_freq[e], hy_decay[e], hy_bias[e])
                y = jnp.concatenate([ya, yb], axis=-1) @ ev_out_w[e]
            else:
                o = l // 2
                u = h @ od_in_w[o]
                yc, s_f, s_b = retention_mix(u[..., :4 * RET_W], ret_log_decay_fwd[o], ret_log_decay_bwd[o],
                                             ret_gn[o], s0_f[:, o], s0_b[:, o])
                yd = pool_mix(u[..., 4 * RET_W:], pool_w[o], pool_scale[o])
                s_f_out.append(s_f)
                s_b_out.append(s_b)
                y = jnp.concatenate([yc, yd], axis=-1) @ od_out_w[o]
            x = x + g2[..., None, :] * y
            h = modulate(rmsnorm(x, norm_g[l, 2]), sh3, sc3)
            x = x + 0.5 * g3[..., None, :] * swiglu(h, ffn_b_in[l], ffn_b_out[l])
        return rmsnorm(x, final_norm), jnp.stack(s_f_out, axis=1), jnp.stack(s_b_out, axis=1)

    zero_state = jnp.zeros((x_prompt.shape[0], N_ODD, RET_HEADS, RET_DK, RET_DV), F32)
    y_prompt, st_f, st_b = trunk(x_prompt, c_ctx, zero_state, zero_state)
    new_state_ret_fwd = st_f.astype(x_prompt.dtype)
    new_state_ret_bwd = st_b.astype(x_prompt.dtype)

    x_lat = x_sample + grid_pos_embed(x_sample.shape[1], x_sample.dtype)[None]
    y_sample, _, _ = trunk(x_lat, c, state_ret_fwd, state_ret_bwd)
    return (y_prompt, y_sample, new_state_ret_fwd, new_state_ret_bwd)
```

```python
import functools
import math

import jax
import jax.numpy as jnp
from jax import lax
from jax.experimental import pallas as pl
from jax.experimental.pallas import tpu as pltpu

F32 = jnp.float32
BF16 = jnp.bfloat16

EPS = 1e-6
N_MOD = 9
GRID_W = 64
POS_BASE = 10000.0
N_GROUPS = 4
HY_BANDS = 16
RET_CHUNK = 128
POOL_WINDOWS = (2, 4, 8, 16)
POOL_PAD = 8
LANE = 128
MOD_ROWS = 16
VMEM_LIMIT = 56 * 1024 * 1024


def _params(*sem):
    return pltpu.CompilerParams(dimension_semantics=sem, vmem_limit_bytes=VMEM_LIMIT)


def _tile(n, pref):
    if n <= pref:
        return n
    t = pref
    while n % t:
        t -= LANE
    assert t > 0, (n, pref)
    return t


def _silu(x):
    return x * (1.0 / (1.0 + jnp.exp(-x)))


def _norm_mod(x, g, shift, scale):
    ms = jnp.mean(x * x, axis=-1, keepdims=True)
    y = x * lax.rsqrt(ms + EPS) * g
    return y * (1.0 + scale) + shift


def _dot(a, b):
    return jnp.dot(a, b, preferred_element_type=F32)


def _mod_body(c_ref, w_ref, b_ref, o_ref):
    s = _silu(c_ref[...]).astype(BF16)
    o_ref[...] = _dot(s, w_ref[...].astype(BF16)) + b_ref[...]


def _mod_call(cond, mod_w, mod_b):
    depth, d, n_out = mod_w.shape
    tn = _tile(n_out, 1024)
    return pl.pallas_call(
        _mod_body,
        grid=(depth, n_out // tn),
        in_specs=[
            pl.BlockSpec((MOD_ROWS, d), lambda l, j: (0, 0)),
            pl.BlockSpec((None, d, tn), lambda l, j: (l, 0, j)),
            pl.BlockSpec((None, 1, tn), lambda l, j: (l, 0, j)),
        ],
        out_specs=pl.BlockSpec((None, MOD_ROWS, tn), lambda l, j: (l, 0, j)),
        out_shape=jax.ShapeDtypeStruct((depth, MOD_ROWS, n_out), F32),
        compiler_params=_params("parallel", "parallel"),
        name="mod_proj",
    )(cond, mod_w, mod_b.reshape(depth, 1, n_out))


def _ffn_body(x_ref, g_ref, sh_ref, sc_ref, gt_ref, wg_ref, wu_ref, wo_ref, o_ref, h_ref, acc_ref):
    j = pl.program_id(1)

    @pl.when(j == 0)
    def _():
        h = _norm_mod(x_ref[...], g_ref[...], sh_ref[...], sc_ref[...])
        h_ref[...] = h.astype(BF16)
        acc_ref[...] = jnp.zeros_like(acc_ref)

    h = h_ref[...]
    gate = _dot(h, wg_ref[...])
    up = _dot(h, wu_ref[...])
    a = (_silu(gate) * up).astype(BF16)
    acc_ref[...] += _dot(a, wo_ref[...])

    @pl.when(j == pl.num_programs(1) - 1)
    def _():
        o_ref[...] = x_ref[...] + 0.5 * gt_ref[...] * acc_ref[...]


def _ffn_call(x, g, shift, scale, gate, w_in, w_out, rows_per_mod):
    t, d = x.shape
    d_ff = w_out.shape[0]
    tm = _tile(t, 512)
    tf = _tile(d_ff, 512)
    nf = d_ff // tf
    mod_spec = pl.BlockSpec((None, 1, d), lambda i, j: ((i * tm) // rows_per_mod, 0, 0))
    return pl.pallas_call(
        _ffn_body,
        grid=(t // tm, nf),
        in_specs=[
            pl.BlockSpec((tm, d), lambda i, j: (i, 0)),
            pl.BlockSpec((1, d), lambda i, j: (0, 0)),
            mod_spec, mod_spec, mod_spec,
            pl.BlockSpec((d, tf), lambda i, j: (0, j)),
            pl.BlockSpec((d, tf), lambda i, j: (0, j + nf)),
            pl.BlockSpec((tf, d), lambda i, j: (j, 0)),
        ],
        out_specs=pl.BlockSpec((tm, d), lambda i, j: (i, 0)),
        out_shape=jax.ShapeDtypeStruct((t, d), F32),
        scratch_shapes=[pltpu.VMEM((tm, d), BF16), pltpu.VMEM((tm, d), F32)],
        compiler_params=_params("parallel", "arbitrary"),
        name="ffn",
    )(x, g, shift, scale, gate, w_in, w_in, w_out)


def _inproj_body(x_ref, g_ref, sh_ref, sc_ref, w_ref, o_ref, h_ref):
    @pl.when(pl.program_id(1) == 0)
    def _():
        h = _norm_mod(x_ref[...], g_ref[...], sh_ref[...], sc_ref[...])
        h_ref[...] = h.astype(BF16)

    o_ref[...] = _dot(h_ref[...], w_ref[...])


def _inproj_call(x, g, shift, scale, w, rows_per_mod):
    t, d = x.shape
    n_out = w.shape[1]
    tm = _tile(t, 512)
    tn = _tile(n_out, 512)
    mod_spec = pl.BlockSpec((None, 1, d), lambda i, j: ((i * tm) // rows_per_mod, 0, 0))
    return pl.pallas_call(
        _inproj_body,
        grid=(t // tm, n_out // tn),
        in_specs=[
            pl.BlockSpec((tm, d), lambda i, j: (i, 0)),
            pl.BlockSpec((1, d), lambda i, j: (0, 0)),
            mod_spec, mod_spec,
            pl.BlockSpec((d, tn), lambda i, j: (0, j)),
        ],
        out_specs=pl.BlockSpec((tm, tn), lambda i, j: (i, j)),
        out_shape=jax.ShapeDtypeStruct((t, n_out), F32),
        scratch_shapes=[pltpu.VMEM((tm, d), BF16)],
        compiler_params=_params("parallel", "arbitrary"),
        name="mix_in_proj",
    )(x, g, shift, scale, w)


def _outproj_body(x_ref, ya_ref, yb_ref, wa_ref, wb_ref, gt_ref, o_ref):
    y = _dot(ya_ref[...], wa_ref[...]) + _dot(yb_ref[...], wb_ref[...])
    o_ref[...] = x_ref[...] + gt_ref[...] * y


def _outproj_call(x, ya, yb, w, gate, rows_per_mod):
    t, d = x.shape
    half = ya.shape[1]
    tm = _tile(t, 512)
    tn = _tile(d, 1024)
    return pl.pallas_call(
        _outproj_body,
        grid=(t // tm, d // tn),
        in_specs=[
            pl.BlockSpec((tm, tn), lambda i, j: (i, j)),
            pl.BlockSpec((tm, half), lambda i, j: (i, 0)),
            pl.BlockSpec((tm, half), lambda i, j: (i, 0)),
            pl.BlockSpec((half, tn), lambda i, j: (0, j)),
            pl.BlockSpec((half, tn), lambda i, j: (1, j)),
            pl.BlockSpec((None, 1, tn), lambda i, j: ((i * tm) // rows_per_mod, 0, j)),
        ],
        out_specs=pl.BlockSpec((tm, tn), lambda i, j: (i, j)),
        out_shape=jax.ShapeDtypeStruct((t, d), F32),
        compiler_params=_params("parallel", "parallel"),
        name="mix_out_proj",
    )(x, ya, yb, w, w, gate)


def _fnet_chan_body(gc, x_ref, cs_ref, z_ref):
    half = N_GROUPS * gc
    for g in range(N_GROUPS):
        xg = x_ref[:, g * gc:(g + 1) * gc].astype(BF16)
        z = _dot(xg, cs_ref[...])
        z_ref[:, g * gc:(g + 1) * gc] = z[:, :gc].astype(BF16)
        z_ref[:, half + g * gc:half + (g + 1) * gc] = z[:, gc:].astype(BF16)


def _fnet_chan_call(u3, cs, width):
    b, n, _ = u3.shape
    gc = width // N_GROUPS
    tr = _tile(n, 512)
    return pl.pallas_call(
        functools.partial(_fnet_chan_body, gc),
        grid=(b, n // tr),
        in_specs=[
            pl.BlockSpec((None, tr, width), lambda bi, i: (bi, i, 0)),
            pl.BlockSpec((gc, 2 * gc), lambda bi, i: (0, 0)),
        ],
        out_specs=pl.BlockSpec((None, tr, 2 * width), lambda bi, i: (bi, i, 0)),
        out_shape=jax.ShapeDtypeStruct((b, n, 2 * width), BF16),
        compiler_params=_params("parallel", "parallel"),
        name="fnet_chan_dft",
    )(u3, cs)


def _fnet_seq_body(gc, scale, c_ref, s_ref, z_ref, w_ref, o_ref, acc_ref):
    k = pl.program_id(2)
    half = N_GROUPS * gc

    @pl.when(k == 0)
    def _():
        acc_ref[...] = jnp.zeros_like(acc_ref)

    acc_ref[...] += _dot(c_ref[...], z_ref[:, :half]) - _dot(s_ref[...], z_ref[:, half:])

    @pl.when(k == pl.num_programs(2) - 1)
    def _():
        for g in range(N_GROUPS):
            r = (acc_ref[:, g * gc:(g + 1) * gc] * scale).astype(BF16)
            o_ref[:, g * gc:(g + 1) * gc] = _dot(r, w_ref[g]).astype(BF16)


def _fnet_seq_call(cn, sn, z, w):
    b, n, two_w = z.shape
    width = two_w // 2
    gc = width // N_GROUPS
    tm = _tile(n, 512)
    tk = _tile(n, 512)
    scale = 1.0 / math.sqrt(n * gc)
    return pl.pallas_call(
        functools.partial(_fnet_seq_body, gc, scale),
        grid=(b, n // tm, n // tk),
        in_specs=[
            pl.BlockSpec((tm, tk), lambda bi, i, k: (i, k)),
            pl.BlockSpec((tm, tk), lambda bi, i, k: (i, k)),
            pl.BlockSpec((None, tk, two_w), lambda bi, i, k: (bi, k, 0)),
            pl.BlockSpec((N_GROUPS, gc, gc), lambda bi, i, k: (0, 0, 0)),
        ],
        out_specs=pl.BlockSpec((None, tm, width), lambda bi, i, k: (bi, i, 0)),
        out_shape=jax.ShapeDtypeStruct((b, n, width), BF16),
        scratch_shapes=[pltpu.VMEM((tm, width), F32)],
        compiler_params=_params("parallel", "parallel", "arbitrary"),
        name="fnet_seq_dft",
    )(cn, sn, z, w)


def _hy_pre_body(n, u0_ref, u1_ref, u2_ref, w0_ref, w1_ref, w2_ref, b0_ref, b1_ref, b2_ref,
                 x0_ref, vv_ref, vn_ref):
    row = lax.broadcasted_iota(jnp.int32, u0_ref.shape, 0)

    def conv(u_ref, w_ref, b_ref):
        u = u_ref[...]
        prev = jnp.where(row == 0, 0.0, pltpu.roll(u, 1, 0))
        nxt = jnp.where(row == n - 1, 0.0, pltpu.roll(u, n - 1, 0))
        return prev * w_ref[0:1, :] + u * w_ref[1:2, :] + nxt * w_ref[2:3, :] + b_ref[...]

    x0_ref[...] = conv(u0_ref, w0_ref, b0_ref)
    vv = conv(u2_ref, w2_ref, b2_ref) * conv(u1_ref, w1_ref, b1_ref)
    vv_ref[...] = vv
    vn_ref[...] = jnp.sum(jnp.where(row % 2 == 0, vv, -vv), axis=0, keepdims=True)


def _hy_pre_call(u3, col0, width, conv_w, conv_b):
    b, n, _ = u3.shape
    tc = _tile(width, LANE)
    nct = width // tc

    def u_spec(s):
        return pl.BlockSpec((None, n, tc), lambda bi, c: (bi, 0, (col0 + s * width) // tc + c))

    def w_spec(s):
        return pl.BlockSpec((3, tc), lambda bi, c: (0, s * nct + c))

    def b_spec(s):
        return pl.BlockSpec((1, tc), lambda bi, c: (0, s * nct + c))

    out3 = pl.BlockSpec((None, n, tc), lambda bi, c: (bi, 0, c))
    return pl.pallas_call(
        functools.partial(_hy_pre_body, n),
        grid=(b, nct),
        in_specs=[u_spec(0), u_spec(1), u_spec(2), w_spec(0), w_spec(1), w_spec(2),
                  b_spec(0), b_spec(1), b_spec(2)],
        out_specs=[out3, out3, pl.BlockSpec((None, 1, tc), lambda bi, c: (bi, 0, c))],
        out_shape=[jax.ShapeDtypeStruct((b, n, width), F32),
                   jax.ShapeDtypeStruct((b, n, width), F32),
                   jax.ShapeDtypeStruct((b, 1, width), F32)],
        compiler_params=_params("parallel", "parallel"),
        name="hyena_short_conv",
    )(u3, u3, u3, conv_w, conv_w, conv_w, conv_b, conv_b, conv_b)


def _hy_filter_body(z_ref, w1_ref, b1_ref, w2_ref, b2_ref, fr_ref, wf_ref, wb_ref, dec_ref,
                    hs_ref, hd_ref, tn_ref):
    hp = lax.Precision.HIGHEST
    z = z_ref[...]
    fr = fr_ref[...]
    h = jnp.sin(fr * (jnp.dot(z, w1_ref[...], precision=hp, preferred_element_type=F32) + b1_ref[...]))
    h = jnp.sin(fr * (jnp.dot(h, w2_ref[...], precision=hp, preferred_element_type=F32) + b2_ref[...]))
    window = jnp.exp(-z[:, 0:1] * jnp.abs(dec_ref[...]))
    hf = jnp.dot(h, wf_ref[...], precision=hp, preferred_element_type=F32) * window
    hb = jnp.dot(h, wb_ref[...], precision=hp, preferred_element_type=F32) * window
    row = lax.broadcasted_iota(jnp.int32, hf.shape, 0)
    hb = jnp.where(row == 0, 0.0, hb)
    l1 = jnp.sum(jnp.abs(hf), axis=0, keepdims=True) + jnp.sum(jnp.abs(hb), axis=0, keepdims=True)
    hs = (hf + hb) / l1
    hd = (hb - hf) / l1
    hs_ref[...] = hs.astype(BF16)
    hd_ref[...] = hd.astype(BF16)
    tn_ref[...] = jnp.sum(jnp.where(row % 2 == 0, hs, -hs), axis=0, keepdims=True)


def _hy_filter_call(z, w1, b1, w2, b2, freq, w_out, decay):
    n, kz = z.shape
    hid = w1.shape[1]
    width = decay.shape[1]
    tc = _tile(width, 256)
    nct = width // tc
    full = lambda a: pl.BlockSpec(a.shape, lambda c: (0, 0))
    col = pl.BlockSpec((n, tc), lambda c: (0, c))
    vec = pl.BlockSpec((1, tc), lambda c: (0, c))
    return pl.pallas_call(
        _hy_filter_body,
        grid=(nct,),
        in_specs=[full(z), full(w1), full(b1), full(w2), full(b2), full(freq),
                  pl.BlockSpec((hid, tc), lambda c: (0, c)),
                  pl.BlockSpec((hid, tc), lambda c: (0, c + nct)),
                  vec],
        out_specs=[col, col, vec],
        out_shape=[jax.ShapeDtypeStruct((n, width), BF16),
                   jax.ShapeDtypeStruct((n, width), BF16),
                   jax.ShapeDtypeStruct((1, width), F32)],
        compiler_params=_params("parallel"),
        name="hyena_filter",
    )(z, w1, b1, w2, b2, freq, w_out, w_out, decay)


def _hy_spec_body(c_ref, s_ref, hs_ref, hd_ref, tre_ref, tim_ref):
    @pl.when(pl.program_id(1) == 0)
    def _():
        tre_ref[...] = jnp.zeros_like(tre_ref)
        tim_ref[...] = jnp.zeros_like(tim_ref)

    tre_ref[...] += _dot(c_ref[...], hs_ref[...])
    tim_ref[...] += _dot(s_ref[...], hd_ref[...])


def _hy_spec_call(cq, sq, hs, hd):
    n, width = hs.shape
    tm = _tile(n, 512)
    tk = _tile(n, 512)
    mat = pl.BlockSpec((tm, tk), lambda i, k: (i, k))
    rhs = pl.BlockSpec((tk, width), lambda i, k: (k, 0))
    out = pl.BlockSpec((tm, width), lambda i, k: (i, 0))
    return pl.pallas_call(
        _hy_spec_body,
        grid=(n // tm, n // tk),
        in_specs=[mat, mat, rhs, rhs],
        out_specs=[out, out],
        out_shape=[jax.ShapeDtypeStruct((n, width), F32)] * 2,
        compiler_params=_params("parallel", "arbitrary"),
        name="hyena_filter_spectrum",
    )(cq, sq, hs, hd)


def _hy_fwd_body(n, width, c_ref, s_ref, v_ref, tre_ref, tim_ref, p_ref, ar_ref, ai_ref):
    k = pl.program_id(2)

    @pl.when(k == 0)
    def _():
        ar_ref[...] = jnp.zeros_like(ar_ref)
        ai_ref[...] = jnp.zeros_like(ai_ref)

    v = v_ref[...].astype(BF16)
    ar_ref[...] += _dot(c_ref[...], v)
    ai_ref[...] += _dot(s_ref[...], v)

    @pl.when(k == pl.num_programs(2) - 1)
    def _():
        tm = ar_ref.shape[0]
        freq = lax.broadcasted_iota(jnp.int32, (tm, 1), 0) + pl.program_id(1) * tm
        wk = jnp.where(freq == 0, 0.5 / n, 1.0 / n)
        ar, ai, tre, tim = ar_ref[...], ai_ref[...], tre_ref[...], tim_ref[...]
        p_ref[:, :width] = ((ar * tre + ai * tim) * wk).astype(BF16)
        p_ref[:, width:] = ((ar * tim - ai * tre) * wk).astype(BF16)


def _hy_fwd_call(cq, sq, vv, tre, tim):
    b, n, width = vv.shape
    tm = _tile(n, 512)
    tk = _tile(n, 512)
    mat = pl.BlockSpec((tm, tk), lambda bi, i, k: (i, k))
    spec = pl.BlockSpec((tm, width), lambda bi, i, k: (i, 0))
    return pl.pallas_call(
        functools.partial(_hy_fwd_body, n, width),
        grid=(b, n // tm, n // tk),
        in_specs=[mat, mat, pl.BlockSpec((None, tk, width), lambda bi, i, k: (bi, k, 0)), spec, spec],
        out_specs=pl.BlockSpec((None, tm, 2 * width), lambda bi, i, k: (bi, i, 0)),
        out_shape=jax.ShapeDtypeStruct((b, n, 2 * width), BF16),
        scratch_shapes=[pltpu.VMEM((tm, width), F32), pltpu.VMEM((tm, width), F32)],
        compiler_params=_params("parallel", "parallel", "arbitrary"),
        name="hyena_fwd_dft",
    )(cq, sq, vv, tre, tim)


def _hy_inv_body(n, width, c_ref, s_ref, p_ref, x0_ref, vv_ref, vn_ref, tn_ref, bias_ref, o_ref, acc_ref):
    k = pl.program_id(2)

    @pl.when(k == 0)
    def _():
        acc_ref[...] = jnp.zeros_like(acc_ref)

    acc_ref[...] += _dot(c_ref[...], p_ref[:, :width]) - _dot(s_ref[...], p_ref[:, width:])

    @pl.when(k == pl.num_programs(2) - 1)
    def _():
        tm = acc_ref.shape[0]
        pos = lax.broadcasted_iota(jnp.int32, (tm, 1), 0) + pl.program_id(1) * tm
        sign = jnp.where(pos % 2 == 0, 1.0, -1.0)
        nyq = (vn_ref[...] * tn_ref[...]) * (0.5 / n)
        conv = acc_ref[...] + sign * nyq
        o_ref[...] = (x0_ref[...] * (conv + vv_ref[...] * bias_ref[...])).astype(BF16)


def _hy_inv_call(cq, sq, p, x0, vv, vn, tn, bias):
    b, n, width = vv.shape
    tm = _tile(n, 512)
    tk = _tile(n, 512)
    mat = pl.BlockSpec((tm, tk), lambda bi, i, k: (i, k))
    tok = pl.BlockSpec((None, tm, width), lambda bi, i, k: (bi, i, 0))
    vec = pl.BlockSpec((1, width), lambda bi, i, k: (0, 0))
    return pl.pallas_call(
        functools.partial(_hy_inv_body, n, width),
        grid=(b, n // tm, n // tk),
        in_specs=[mat, mat, pl.BlockSpec((None, tk, 2 * width), lambda bi, i, k: (bi, k, 0)),
                  tok, tok, pl.BlockSpec((None, 1, width), lambda bi, i, k: (bi, 0, 0)), vec, vec],
        out_specs=tok,
        out_shape=jax.ShapeDtypeStruct((b, n, width), BF16),
        scratch_shapes=[pltpu.VMEM((tm, width), F32)],
        compiler_params=_params("parallel", "parallel", "arbitrary"),
        name="hyena_inv_dft",
    )(cq, sq, p, x0, vv, vn, tn, bias)


def _ret_scan_body(dk, lgf_ref, lgb_ref, qf_ref, kf_ref, vf_ref, qb_ref, kb_ref, vb_ref,
                   s0f_ref, s0b_ref, of_ref, ob_ref, sfo_ref, sbo_ref, sf_ref, sb_ref):
    h = pl.program_id(1)
    c = pl.program_id(2)
    ch = RET_CHUNK

    @pl.when(c == 0)
    def _():
        sf_ref[...] = s0f_ref[...]
        sb_ref[...] = s0b_ref[...]

    lgf = lgf_ref[h]
    lgb = lgb_ref[h]
    ii = lax.broadcasted_iota(jnp.int32, (ch, ch), 0)
    jj = lax.broadcasted_iota(jnp.int32, (ch, ch), 1)
    diff = (ii - jj).astype(F32)
    idx = lax.broadcasted_iota(jnp.int32, (ch, 1), 0).astype(F32)
    k_scale = dk ** -0.5

    def direction(lg, lag, q_pow, k_pow, q_ref, k_ref, v_ref, s_ref, o_ref):
        inner_decay = jnp.where(lag >= 0, jnp.exp(lg * jnp.maximum(lag, 0.0)), 0.0)
        q = q_ref[...].astype(BF16)
        k = k_ref[...] * k_scale
        v = v_ref[...].astype(BF16)
        scores = lax.dot_general(q, k.astype(BF16), (((1,), (1,)), ((), ())),
                                 preferred_element_type=F32) * inner_decay
        inner = _dot(scores.astype(BF16), v)
        s = s_ref[...]
        cross = _dot(q, s.astype(BF16)) * jnp.exp(lg * q_pow)
        o_ref[...] = inner + cross
        kd = (k * jnp.exp(lg * k_pow)).T.astype(BF16)
        chunk_decay = jnp.exp(lg * jnp.full((1, 1), float(ch), F32))
        s_ref[...] = s * chunk_decay + _dot(kd, v)

    direction(lgf, diff, idx + 1.0, (ch - 1.0) - idx, qf_ref, kf_ref, vf_ref, sf_ref, of_ref)
    direction(lgb, -diff, float(ch) - idx, idx, qb_ref, kb_ref, vb_ref, sb_ref, ob_ref)

    @pl.when(c == pl.num_programs(2) - 1)
    def _():
        sfo_ref[...] = sf_ref[...]
        sbo_ref[...] = sb_ref[...]


def _ret_scan_call(u3, width, lgf, lgb, s0f, s0b):
    b, n, _ = u3.shape
    dk = width // N_GROUPS
    nc = n // RET_CHUNK

    def tok(part, rev):
        if rev:
            return pl.BlockSpec((None, RET_CHUNK, dk), lambda bi, h, c: (bi, nc - 1 - c, part * N_GROUPS + h))
        return pl.BlockSpec((None, RET_CHUNK, dk), lambda bi, h, c: (bi, c, part * N_GROUPS + h))

    def out_tok(rev):
        if rev:
            return pl.BlockSpec((None, RET_CHUNK, dk), lambda bi, h, c: (bi, nc - 1 - c, h))
        return pl.BlockSpec((None, RET_CHUNK, dk), lambda bi, h, c: (bi, c, h))

    state = pl.BlockSpec((None, None, dk, dk), lambda bi, h, c: (bi, h, 0, 0))
    smem = pl.BlockSpec(memory_space=pltpu.SMEM)
    return pl.pallas_call(
        functools.partial(_ret_scan_body, dk),
        grid=(b, N_GROUPS, nc),
        in_specs=[smem, smem, tok(0, False), tok(1, False), tok(2, False),
                  tok(0, True), tok(1, True), tok(2, True), state, state],
        out_specs=[out_tok(False), out_tok(True), state, state],
        out_shape=[jax.ShapeDtypeStruct((b, n, width), F32),
                   jax.ShapeDtypeStruct((b, n, width), F32),
                   jax.ShapeDtypeStruct((b, N_GROUPS, dk, dk), F32),
                   jax.ShapeDtypeStruct((b, N_GROUPS, dk, dk), F32)],
        scratch_shapes=[pltpu.VMEM((dk, dk), F32), pltpu.VMEM((dk, dk), F32)],
        compiler_params=_params("parallel", "parallel", "arbitrary"),
        name="retention_scan",
    )(lgf, lgb, u3, u3, u3, u3, u3, u3, s0f, s0b)


def _ret_post_body(dv, of_ref, ob_ref, g_ref, gn_ref, y_ref):
    for h in range(N_GROUPS):
        sl = slice(h * dv, (h + 1) * dv)
        o = of_ref[:, sl] + ob_ref[:, sl]
        mu = jnp.mean(o, axis=-1, keepdims=True)
        var = jnp.mean(jnp.square(o - mu), axis=-1, keepdims=True)
        on = (o - mu) * lax.rsqrt(var + EPS) * gn_ref[:, sl]
        y_ref[:, sl] = (_silu(g_ref[:, sl]) * on).astype(BF16)


def _ret_post_call(o_f, o_b, u3, gn):
    b, n, width = o_f.shape
    tr = _tile(n, 512)
    tok = pl.BlockSpec((None, tr, width), lambda bi, i: (bi, i, 0))
    return pl.pallas_call(
        functools.partial(_ret_post_body, width // N_GROUPS),
        grid=(b, n // tr),
        in_specs=[tok, tok, pl.BlockSpec((None, tr, width), lambda bi, i: (bi, i, 3)),
                  pl.BlockSpec((1, width), lambda bi, i: (0, 0))],
        out_specs=tok,
        out_shape=jax.ShapeDtypeStruct((b, n, width), BF16),
        compiler_params=_params("parallel", "parallel"),
        name="retention_norm_gate",
    )(o_f, o_b, u3, gn)


def _pool_body(n, x_ref, w_ref, sc_ref, y_ref):
    g = pl.program_id(1)
    gc = x_ref.shape[1]
    n_pad = n + 2 * POOL_PAD
    pos = lax.broadcasted_iota(jnp.int32, (n, 1), 0)

    for gi, win in enumerate(POOL_WINDOWS):
        @pl.when(g == gi)
        def _(win=win):
            x = x_ref[...]
            zeros = jnp.zeros((POOL_PAD, gc), F32)
            s = jnp.concatenate([zeros, x, zeros], axis=0)
            span = 1
            while span < win:
                s = s + pltpu.roll(s, span, 0)
                span *= 2
            total = pltpu.roll(s, n_pad - (POOL_PAD + win // 2 - 1), 0)[:n]
            lo = jnp.clip(pos - win // 2, 0, n - 1)
            hi = jnp.clip(pos - win // 2 + win - 1, 0, n - 1)
            count = (hi - lo + 1).astype(F32)
            p = total / count - x
            y = _dot(p.astype(BF16), w_ref[...]) * sc_ref[...]
            y_ref[...] = y.astype(BF16)


def _pool_call(u3, col0, width, w, scale):
    b, n, _ = u3.shape
    gc = width // N_GROUPS
    assert max(POOL_WINDOWS) // 2 <= POOL_PAD
    return pl.pallas_call(
        functools.partial(_pool_body, n),
        grid=(b, N_GROUPS),
        in_specs=[pl.BlockSpec((None, n, gc), lambda bi, g: (bi, 0, col0 // gc + g)),
                  pl.BlockSpec((None, gc, gc), lambda bi, g: (g, 0, 0)),
                  pl.BlockSpec((1, gc), lambda bi, g: (0, g))],
        out_specs=pl.BlockSpec((None, n, gc), lambda bi, g: (bi, 0, g)),
        out_shape=jax.ShapeDtypeStruct((b, n, width), BF16),
        compiler_params=_params("parallel", "parallel"),
        name="pool_mix",
    )(u3, w, scale)


def _add_pos_body(x_ref, p_ref, o_ref):
    o_ref[...] = x_ref[...] + p_ref[...]


def _add_pos_call(x3, pos):
    b, n, d = x3.shape
    tr = _tile(n, 512)
    tok = pl.BlockSpec((None, tr, d), lambda bi, i: (bi, i, 0))
    return pl.pallas_call(
        _add_pos_body,
        grid=(b, n // tr),
        in_specs=[tok, pl.BlockSpec((tr, d), lambda bi, i: (i, 0))],
        out_specs=tok,
        out_shape=jax.ShapeDtypeStruct((b, n, d), F32),
        compiler_params=_params("parallel", "parallel"),
        name="add_pos_embed",
    )(x3, pos)


def _final_norm_body(x_ref, g_ref, o_ref):
    x = x_ref[...]
    ms = jnp.mean(x * x, axis=-1, keepdims=True)
    o_ref[...] = x * lax.rsqrt(ms + EPS) * g_ref[...]


def _final_norm_call(x, g):
    t, d = x.shape
    tm = _tile(t, 512)
    return pl.pallas_call(
        _final_norm_body,
        grid=(t // tm,),
        in_specs=[pl.BlockSpec((tm, d), lambda i: (i, 0)), pl.BlockSpec((1, d), lambda i: (0, 0))],
        out_specs=pl.BlockSpec((tm, d), lambda i: (i, 0)),
        out_shape=jax.ShapeDtypeStruct((t, d), F32),
        compiler_params=_params("parallel"),
        name="final_norm",
    )(x, g)


def _dft_tables(n, period):
    k = jnp.arange(n, dtype=jnp.int32)
    m = (k[:, None] * k[None, :]) % period
    ang = m.astype(F32) * (2.0 * math.pi / period)
    return jnp.cos(ang).astype(BF16), jnp.sin(ang).astype(BF16)


def _grid_pos_embed(n_tokens, d):
    rows = n_tokens // GRID_W
    rr, cc = jnp.meshgrid(jnp.arange(rows, dtype=F32), jnp.arange(GRID_W, dtype=F32), indexing='ij')
    rr = rr.reshape(-1)[:, None]
    cc = cc.reshape(-1)[:, None]
    quarter = d // 4
    omega = 1.0 / (POS_BASE ** (jnp.arange(quarter, dtype=F32) / quarter))
    ar, ac = rr * omega, cc * omega
    return jnp.concatenate([jnp.sin(ar), jnp.cos(ar), jnp.sin(ac), jnp.cos(ac)], axis=-1)


def _hyena_pos_features(n):
    t_idx = jnp.arange(n, dtype=F32)
    t = jnp.linspace(0.0, 1.0, n, dtype=F32)
    bands = jnp.linspace(1e-4, HY_BANDS - 1, HY_BANDS, dtype=F32)
    ang = (2.0 * math.pi / n) * t_idx[:, None] * bands[None, :]
    z = jnp.concatenate([t[:, None], jnp.cos(ang), jnp.sin(ang)], axis=-1)
    return jnp.pad(z, ((0, 0), (0, LANE - z.shape[1])))


def _pad2(a, rows, cols):
    return jnp.pad(a, ((0, rows - a.shape[0]), (0, cols - a.shape[1])))


def kernel(x_prompt, x_sample, state_ret_fwd, state_ret_bwd, c, c_ctx, norm_g, mod_w, mod_b, ffn_a_in, ffn_a_out, ffn_b_in, ffn_b_out, ev_in_w, ev_out_w, fnet_w, hy_conv_w, hy_conv_b, hy_w1, hy_b1, hy_w2, hy_b2, hy_w_out, hy_freq, hy_decay, hy_bias, od_in_w, od_out_w, ret_log_decay_fwd, ret_log_decay_bwd, ret_gn, pool_w, pool_scale, final_norm):
    depth, d = norm_g.shape[0], norm_g.shape[2]
    half = d // 2
    dec_b = x_sample.shape[0]
    assert dec_b + 1 <= MOD_ROWS

    cond = jnp.concatenate([c, c_ctx[None, :]], axis=0)
    cond = jnp.pad(cond, ((0, MOD_ROWS - cond.shape[0]), (0, 0)))
    mod = _mod_call(cond, mod_w, mod_b).reshape(depth, MOD_ROWS, N_MOD, 1, d)

    bf = lambda a: a.astype(BF16)
    ffn_a_in, ffn_a_out, ffn_b_in, ffn_b_out = bf(ffn_a_in), bf(ffn_a_out), bf(ffn_b_in), bf(ffn_b_out)
    ev_in_w, ev_out_w, od_in_w, od_out_w = bf(ev_in_w), bf(ev_out_w), bf(od_in_w), bf(od_out_w)
    fnet_w, pool_w = bf(fnet_w), bf(pool_w)

    hid = hy_w1.shape[2]
    n_even = hy_w1.shape[0]
    hy_w1p = jnp.stack([_pad2(hy_w1[e], LANE, LANE) for e in range(n_even)])
    hy_w2p = jnp.stack([_pad2(hy_w2[e], LANE, LANE) for e in range(n_even)])
    hy_woutp = jnp.pad(hy_w_out, ((0, 0), (0, LANE - hid), (0, 0)))
    padv = lambda a: jnp.pad(a, ((0, 0), (0, LANE - hid)))[:, None, :]
    hy_b1p, hy_b2p, hy_freqp = padv(hy_b1), padv(hy_b2), padv(hy_freq)

    gc = half // N_GROUPS
    cs_chan = jnp.concatenate(_dft_tables(gc, gc), axis=1)

    def trunk(x3, mod_rows, s0_f, s0_b):
        b, n, _ = x3.shape
        t = b * n
        rows_per_mod = n if mod_rows.stop - mod_rows.start > 1 else t
        x = x3.reshape(t, d)
        cn, sn = _dft_tables(n, n)
        cq, sq = _dft_tables(n, 2 * n)
        z_pos = _hyena_pos_features(n)
        s_f_out, s_b_out = [], []
        for l in range(depth):
            m = [mod[l, mod_rows, i] for i in range(N_MOD)]
            x = _ffn_call(x, norm_g[l, 0][None], m[0], m[1], m[2], ffn_a_in[l], ffn_a_out[l], rows_per_mod)
            if l % 2 == 0:
                e = l // 2
                u3 = _inproj_call(x, norm_g[l, 1][None], m[3], m[4], ev_in_w[e], rows_per_mod).reshape(b, n, -1)
                zc = _fnet_chan_call(u3, cs_chan, half)
                ya = _fnet_seq_call(cn, sn, zc, fnet_w[e])
                x0, vv, vn = _hy_pre_call(u3, half, half, hy_conv_w[e], hy_conv_b[e][None])
                hs, hd, tn = _hy_filter_call(z_pos, hy_w1p[e], hy_b1p[e], hy_w2p[e], hy_b2p[e], hy_freqp[e],
                                             hy_woutp[e], hy_decay[e][None])
                tre, tim = _hy_spec_call(cq, sq, hs, hd)
                p = _hy_fwd_call(cq, sq, vv, tre, tim)
                yb = _hy_inv_call(cq, sq, p, x0, vv, vn, tn, hy_bias[e][None])
                w_out = ev_out_w[e]
            else:
                o = l // 2
                u3 = _inproj_call(x, norm_g[l, 1][None], m[3], m[4], od_in_w[o], rows_per_mod).reshape(b, n, -1)
                o_f, o_b, s_f, s_b = _ret_scan_call(u3, half, ret_log_decay_fwd[o], ret_log_decay_bwd[o],
                                                    s0_f[:, o], s0_b[:, o])
                ya = _ret_post_call(o_f, o_b, u3, ret_gn[o][None])
                yb = _pool_call(u3, 4 * half, half, pool_w[o], pool_scale[o][None])
                s_f_out.append(s_f)
                s_b_out.append(s_b)
                w_out = od_out_w[o]
            x = _outproj_call(x, ya.reshape(t, half), yb.reshape(t, half), w_out, m[5], rows_per_mod)
            x = _ffn_call(x, norm_g[l, 2][None], m[6], m[7], m[8], ffn_b_in[l], ffn_b_out[l], rows_per_mod)
        y = _final_norm_call(x, final_norm[None]).reshape(b, n, d)
        return y, jnp.stack(s_f_out, axis=1), jnp.stack(s_b_out, axis=1)

    zero_state = jnp.zeros((x_prompt.shape[0],) + state_ret_fwd.shape[1:], F32)
    y_prompt, st_f, st_b = trunk(x_prompt, slice(dec_b, dec_b + 1), zero_state, zero_state)

    x_lat = _add_pos_call(x_sample, _grid_pos_embed(x_sample.shape[1], d))
    y_sample, _, _ = trunk(x_lat, slice(0, dec_b), state_ret_fwd, state_ret_bwd)
    return (y_prompt, y_sample, st_f, st_b)
```

```python
import functools
import math

import jax
import jax.numpy as jnp
from jax import lax
from jax.experimental import pallas as pl
from jax.experimental.pallas import tpu as pltpu

F32 = jnp.float32
BF16 = jnp.bfloat16

EPS = 1e-6
N_MOD = 9
GRID_W = 64
POS_BASE = 10000.0
N_GROUPS = 4
HY_BANDS = 16
RET_CHUNK = 128
POOL_WINDOWS = (2, 4, 8, 16)
POOL_PAD = 8
LANE = 128
MOD_ROWS = 16
VMEM_LIMIT = 56 * 1024 * 1024


def _params(*sem):
    return pltpu.CompilerParams(dimension_semantics=sem, vmem_limit_bytes=VMEM_LIMIT)


def _tile(n, pref):
    if n <= pref:
        return n
    t = pref
    while n % t:
        t -= LANE
    assert t > 0, (n, pref)
    return t


def _silu(x):
    return x * (1.0 / (1.0 + jnp.exp(-x)))


def _norm_mod(x, g, shift, scale):
    ms = jnp.mean(x * x, axis=-1, keepdims=True)
    y = x * lax.rsqrt(ms + EPS) * g
    return y * (1.0 + scale) + shift


def _dot(a, b):
    return jnp.dot(a, b, preferred_element_type=F32)


def _mod_body(c_ref, w_ref, b_ref, o_ref):
    s = _silu(c_ref[...]).astype(BF16)
    o_ref[...] = _dot(s, w_ref[...].astype(BF16)) + b_ref[...]


def _mod_call(cond, mod_w, mod_b):
    depth, d, n_out = mod_w.shape
    tn = _tile(n_out, 1024)
    return pl.pallas_call(
        _mod_body,
        grid=(depth, n_out // tn),
        in_specs=[
            pl.BlockSpec((MOD_ROWS, d), lambda l, j: (0, 0)),
            pl.BlockSpec((None, d, tn), lambda l, j: (l, 0, j)),
            pl.BlockSpec((None, 1, tn), lambda l, j: (l, 0, j)),
        ],
        out_specs=pl.BlockSpec((None, MOD_ROWS, tn), lambda l, j: (l, 0, j)),
        out_shape=jax.ShapeDtypeStruct((depth, MOD_ROWS, n_out), F32),
        compiler_params=_params("parallel", "parallel"),
        name="mod_proj",
    )(cond, mod_w, mod_b.reshape(depth, 1, n_out))


def _ffn_body(final, x_ref, g_ref, sh_ref, sc_ref, gt_ref, wg_ref, wu_ref, wo_ref, *rest):
    if final:
        fg_ref, o_ref, h_ref, acc_ref = rest
    else:
        o_ref, h_ref, acc_ref = rest
    j = pl.program_id(1)

    @pl.when(j == 0)
    def _():
        h = _norm_mod(x_ref[...], g_ref[...], sh_ref[...], sc_ref[...])
        h_ref[...] = h.astype(BF16)
        acc_ref[...] = jnp.zeros_like(acc_ref)

    h = h_ref[...]
    gate = _dot(h, wg_ref[...])
    up = _dot(h, wu_ref[...])
    a = (_silu(gate) * up).astype(BF16)
    acc_ref[...] += _dot(a, wo_ref[...])

    @pl.when(j == pl.num_programs(1) - 1)
    def _():
        y = x_ref[...] + 0.5 * gt_ref[...] * acc_ref[...]
        if final:
            ms = jnp.mean(y * y, axis=-1, keepdims=True)
            y = y * lax.rsqrt(ms + EPS) * fg_ref[...]
        o_ref[...] = y


def _ffn_call(x, g, shift, scale, gate, w_in, w_out, layer, rows_per_mod, final_g=None):
    t, d = x.shape
    d_ff = w_out.shape[1]
    tm = _tile(rows_per_mod, 512)
    tf = _tile(d_ff, 512)
    nf = d_ff // tf
    final = final_g is not None
    mod_spec = pl.BlockSpec((None, 1, d), lambda i, j: ((i * tm) // rows_per_mod, 0, 0))
    vec_spec = pl.BlockSpec((1, d), lambda i, j: (0, 0))
    in_specs = [
        pl.BlockSpec((tm, d), lambda i, j: (i, 0)),
        vec_spec, mod_spec, mod_spec, mod_spec,
        pl.BlockSpec((None, d, tf), lambda i, j: (layer, 0, j)),
        pl.BlockSpec((None, d, tf), lambda i, j: (layer, 0, j + nf)),
        pl.BlockSpec((None, tf, d), lambda i, j: (layer, j, 0)),
    ]
    args = [x, g, shift, scale, gate, w_in, w_in, w_out]
    if final:
        in_specs.append(vec_spec)
        args.append(final_g)
    return pl.pallas_call(
        functools.partial(_ffn_body, final),
        grid=(t // tm, nf),
        in_specs=in_specs,
        out_specs=pl.BlockSpec((tm, d), lambda i, j: (i, 0)),
        out_shape=jax.ShapeDtypeStruct((t, d), F32),
        scratch_shapes=[pltpu.VMEM((tm, d), BF16), pltpu.VMEM((tm, d), F32)],
        compiler_params=_params("parallel", "arbitrary"),
        name="ffn",
    )(*args)


def _inproj_body(x_ref, g_ref, sh_ref, sc_ref, w_ref, o_ref, h_ref):
    @pl.when(pl.program_id(1) == 0)
    def _():
        h = _norm_mod(x_ref[...], g_ref[...], sh_ref[...], sc_ref[...])
        h_ref[...] = h.astype(BF16)

    o_ref[...] = _dot(h_ref[...], w_ref[...]).astype(BF16)


def _inproj_call(x, g, shift, scale, w, layer, rows_per_mod):
    t, d = x.shape
    n_out = w.shape[2]
    tm = _tile(rows_per_mod, 1024)
    tn = _tile(n_out, 512)
    mod_spec = pl.BlockSpec((None, 1, d), lambda i, j: ((i * tm) // rows_per_mod, 0, 0))
    return pl.pallas_call(
        _inproj_body,
        grid=(t // tm, n_out // tn),
        in_specs=[
            pl.BlockSpec((tm, d), lambda i, j: (i, 0)),
            pl.BlockSpec((1, d), lambda i, j: (0, 0)),
            mod_spec, mod_spec,
            pl.BlockSpec((None, d, tn), lambda i, j: (layer, 0, j)),
        ],
        out_specs=pl.BlockSpec((tm, tn), lambda i, j: (i, j)),
        out_shape=jax.ShapeDtypeStruct((t, n_out), BF16),
        scratch_shapes=[pltpu.VMEM((tm, d), BF16)],
        compiler_params=_params("parallel", "arbitrary"),
        name="mix_in_proj",
    )(x, g, shift, scale, w)


def _outproj_body(x_ref, ya_ref, yb_ref, wa_ref, wb_ref, gt_ref, o_ref):
    y = _dot(ya_ref[...], wa_ref[...]) + _dot(yb_ref[...], wb_ref[...])
    o_ref[...] = x_ref[...] + gt_ref[...] * y


def _outproj_call(x, ya, yb, w, layer, gate, rows_per_mod):
    t, d = x.shape
    half = ya.shape[1]
    tm = _tile(rows_per_mod, 512)
    return pl.pallas_call(
        _outproj_body,
        grid=(t // tm,),
        in_specs=[
            pl.BlockSpec((tm, d), lambda i: (i, 0)),
            pl.BlockSpec((tm, half), lambda i: (i, 0)),
            pl.BlockSpec((tm, half), lambda i: (i, 0)),
            pl.BlockSpec((None, half, d), lambda i: (layer, 0, 0)),
            pl.BlockSpec((None, half, d), lambda i: (layer, 1, 0)),
            pl.BlockSpec((None, 1, d), lambda i: ((i * tm) // rows_per_mod, 0, 0)),
        ],
        out_specs=pl.BlockSpec((tm, d), lambda i: (i, 0)),
        out_shape=jax.ShapeDtypeStruct((t, d), F32),
        compiler_params=_params("parallel"),
        name="mix_out_proj",
    )(x, ya, yb, w, w, gate)


def _fnet_chan_body(gc, x_ref, cs_ref, z_ref):
    half = N_GROUPS * gc
    for g in range(N_GROUPS):
        xg = x_ref[:, g * gc:(g + 1) * gc].astype(BF16)
        z = _dot(xg, cs_ref[...])
        z_ref[:, g * gc:(g + 1) * gc] = z[:, :gc].astype(BF16)
        z_ref[:, half + g * gc:half + (g + 1) * gc] = z[:, gc:].astype(BF16)


def _fnet_chan_call(u3, cs, width):
    b, n, _ = u3.shape
    gc = width // N_GROUPS
    tr = _tile(n, 512)
    return pl.pallas_call(
        functools.partial(_fnet_chan_body, gc),
        grid=(b, n // tr),
        in_specs=[
            pl.BlockSpec((None, tr, width), lambda bi, i: (bi, i, 0)),
            pl.BlockSpec((gc, 2 * gc), lambda bi, i: (0, 0)),
        ],
        out_specs=pl.BlockSpec((None, tr, 2 * width), lambda bi, i: (bi, i, 0)),
        out_shape=jax.ShapeDtypeStruct((b, n, 2 * width), BF16),
        compiler_params=_params("parallel", "parallel"),
        name="fnet_chan_dft",
    )(u3, cs)


def _fnet_seq_body(gc, scale, c_ref, s_ref, z_ref, w_ref, o_ref, acc_ref):
    k = pl.program_id(2)
    half = N_GROUPS * gc

    @pl.when(k == 0)
    def _():
        acc_ref[...] = jnp.zeros_like(acc_ref)

    acc_ref[...] += _dot(c_ref[...], z_ref[:, :half]) - _dot(s_ref[...], z_ref[:, half:])

    @pl.when(k == pl.num_programs(2) - 1)
    def _():
        for g in range(N_GROUPS):
            r = (acc_ref[:, g * gc:(g + 1) * gc] * scale).astype(BF16)
            o_ref[:, g * gc:(g + 1) * gc] = _dot(r, w_ref[g]).astype(BF16)


def _fnet_seq_call(cn, sn, z, w):
    b, n, two_w = z.shape
    width = two_w // 2
    gc = width // N_GROUPS
    tm = _tile(n, 512)
    tk = _tile(n, 512)
    scale = 1.0 / math.sqrt(n * gc)
    return pl.pallas_call(
        functools.partial(_fnet_seq_body, gc, scale),
        grid=(b, n // tm, n // tk),
        in_specs=[
            pl.BlockSpec((tm, tk), lambda bi, i, k: (i, k)),
            pl.BlockSpec((tm, tk), lambda bi, i, k: (i, k)),
            pl.BlockSpec((None, tk, two_w), lambda bi, i, k: (bi, k, 0)),
            pl.BlockSpec((N_GROUPS, gc, gc), lambda bi, i, k: (0, 0, 0)),
        ],
        out_specs=pl.BlockSpec((None, tm, width), lambda bi, i, k: (bi, i, 0)),
        out_shape=jax.ShapeDtypeStruct((b, n, width), BF16),
        scratch_shapes=[pltpu.VMEM((tm, width), F32)],
        compiler_params=_params("parallel", "parallel", "arbitrary"),
        name="fnet_seq_dft",
    )(cn, sn, z, w)


def _hy_pre_body(n, u0_ref, u1_ref, u2_ref, w0_ref, w1_ref, w2_ref, b0_ref, b1_ref, b2_ref,
                 x0_ref, vv_ref, vn_ref):
    row = lax.broadcasted_iota(jnp.int32, u0_ref.shape, 0)

    def conv(u_ref, w_ref, b_ref):
        u = u_ref[...].astype(F32)
        prev = jnp.where(row == 0, 0.0, pltpu.roll(u, 1, 0))
        nxt = jnp.where(row == n - 1, 0.0, pltpu.roll(u, n - 1, 0))
        return prev * w_ref[0:1, :] + u * w_ref[1:2, :] + nxt * w_ref[2:3, :] + b_ref[...]

    x0_ref[...] = conv(u0_ref, w0_ref, b0_ref)
    vv = conv(u2_ref, w2_ref, b2_ref) * conv(u1_ref, w1_ref, b1_ref)
    vv_ref[...] = vv
    vn_ref[...] = jnp.sum(jnp.where(row % 2 == 0, vv, -vv), axis=0, keepdims=True)


def _hy_pre_call(u3, col0, width, conv_w, conv_b):
    b, n, _ = u3.shape
    tc = _tile(width, LANE)
    nct = width // tc

    def u_spec(s):
        return pl.BlockSpec((None, n, tc), lambda bi, c: (bi, 0, (col0 + s * width) // tc + c))

    def w_spec(s):
        return pl.BlockSpec((3, tc), lambda bi, c: (0, s * nct + c))

    def b_spec(s):
        return pl.BlockSpec((1, tc), lambda bi, c: (0, s * nct + c))

    out3 = pl.BlockSpec((None, n, tc), lambda bi, c: (bi, 0, c))
    return pl.pallas_call(
        functools.partial(_hy_pre_body, n),
        grid=(b, nct),
        in_specs=[u_spec(0), u_spec(1), u_spec(2), w_spec(0), w_spec(1), w_spec(2),
                  b_spec(0), b_spec(1), b_spec(2)],
        out_specs=[out3, out3, pl.BlockSpec((None, 1, tc), lambda bi, c: (bi, 0, c))],
        out_shape=[jax.ShapeDtypeStruct((b, n, width), F32),
                   jax.ShapeDtypeStruct((b, n, width), F32),
                   jax.ShapeDtypeStruct((b, 1, width), F32)],
        compiler_params=_params("parallel", "parallel"),
        name="hyena_short_conv",
    )(u3, u3, u3, conv_w, conv_w, conv_w, conv_b, conv_b, conv_b)


def _hy_filter_body(z_ref, w1_ref, b1_ref, w2_ref, b2_ref, fr_ref, wf_ref, wb_ref, dec_ref,
                    hs_ref, hd_ref, tn_ref):
    hp = lax.Precision.HIGHEST
    z = z_ref[...]
    fr = fr_ref[...]
    h = jnp.sin(fr * (jnp.dot(z, w1_ref[...], precision=hp, preferred_element_type=F32) + b1_ref[...]))
    h = jnp.sin(fr * (jnp.dot(h, w2_ref[...], precision=hp, preferred_element_type=F32) + b2_ref[...]))
    window = jnp.exp(-z[:, 0:1] * jnp.abs(dec_ref[...]))
    hf = jnp.dot(h, wf_ref[...], precision=hp, preferred_element_type=F32) * window
    hb = jnp.dot(h, wb_ref[...], precision=hp, preferred_element_type=F32) * window
    row = lax.broadcasted_iota(jnp.int32, hf.shape, 0)
    hb = jnp.where(row == 0, 0.0, hb)
    l1 = jnp.sum(jnp.abs(hf), axis=0, keepdims=True) + jnp.sum(jnp.abs(hb), axis=0, keepdims=True)
    hs = (hf + hb) / l1
    hd = (hb - hf) / l1
    hs_ref[...] = hs.astype(BF16)
    hd_ref[...] = hd.astype(BF16)
    tn_ref[...] = jnp.sum(jnp.where(row % 2 == 0, hs, -hs), axis=0, keepdims=True)


def _hy_filter_call(z, w1, b1, w2, b2, freq, w_out, decay):
    n, kz = z.shape
    hid = w1.shape[1]
    width = decay.shape[1]
    tc = _tile(width, 256)
    nct = width // tc
    full = lambda a: pl.BlockSpec(a.shape, lambda c: (0, 0))
    col = pl.BlockSpec((n, tc), lambda c: (0, c))
    vec = pl.BlockSpec((1, tc), lambda c: (0, c))
    return pl.pallas_call(
        _hy_filter_body,
        grid=(nct,),
        in_specs=[full(z), full(w1), full(b1), full(w2), full(b2), full(freq),
                  pl.BlockSpec((hid, tc), lambda c: (0, c)),
                  pl.BlockSpec((hid, tc), lambda c: (0, c + nct)),
                  vec],
        out_specs=[col, col, vec],
        out_shape=[jax.ShapeDtypeStruct((n, width), BF16),
                   jax.ShapeDtypeStruct((n, width), BF16),
                   jax.ShapeDtypeStruct((1, width), F32)],
        compiler_params=_params("parallel"),
        name="hyena_filter",
    )(z, w1, b1, w2, b2, freq, w_out, w_out, decay)


def _hy_spec_body(c_ref, s_ref, hs_ref, hd_ref, tre_ref, tim_ref):
    @pl.when(pl.program_id(1) == 0)
    def _():
        tre_ref[...] = jnp.zeros_like(tre_ref)
        tim_ref[...] = jnp.zeros_like(tim_ref)

    tre_ref[...] += _dot(c_ref[...], hs_ref[...])
    tim_ref[...] += _dot(s_ref[...], hd_ref[...])


def _hy_spec_call(cq, sq, hs, hd):
    n, width = hs.shape
    tm = _tile(n, 512)
    tk = _tile(n, 512)
    mat = pl.BlockSpec((tm, tk), lambda i, k: (i, k))
    rhs = pl.BlockSpec((tk, width), lambda i, k: (k, 0))
    out = pl.BlockSpec((tm, width), lambda i, k: (i, 0))
    return pl.pallas_call(
        _hy_spec_body,
        grid=(n // tm, n // tk),
        in_specs=[mat, mat, rhs, rhs],
        out_specs=[out, out],
        out_shape=[jax.ShapeDtypeStruct((n, width), F32)] * 2,
        compiler_params=_params("parallel", "arbitrary"),
        name="hyena_filter_spectrum",
    )(cq, sq, hs, hd)


def _hy_fwd_body(n, width, c_ref, s_ref, v_ref, tre_ref, tim_ref, p_ref, ar_ref, ai_ref):
    k = pl.program_id(2)

    @pl.when(k == 0)
    def _():
        ar_ref[...] = jnp.zeros_like(ar_ref)
        ai_ref[...] = jnp.zeros_like(ai_ref)

    v = v_ref[...].astype(BF16)
    ar_ref[...] += _dot(c_ref[...], v)
    ai_ref[...] += _dot(s_ref[...], v)

    @pl.when(k == pl.num_programs(2) - 1)
    def _():
        tm = ar_ref.shape[0]
        freq = lax.broadcasted_iota(jnp.int32, (tm, 1), 0) + pl.program_id(1) * tm
        wk = jnp.where(freq == 0, 0.5 / n, 1.0 / n)
        ar, ai, tre, tim = ar_ref[...], ai_ref[...], tre_ref[...], tim_ref[...]
        p_ref[:, :width] = ((ar * tre + ai * tim) * wk).astype(BF16)
        p_ref[:, width:] = ((ar * tim - ai * tre) * wk).astype(BF16)


def _hy_fwd_call(cq, sq, vv, tre, tim):
    b, n, width = vv.shape
    tm = _tile(n, 512)
    tk = _tile(n, 512)
    mat = pl.BlockSpec((tm, tk), lambda bi, i, k: (i, k))
    spec = pl.BlockSpec((tm, width), lambda bi, i, k: (i, 0))
    return pl.pallas_call(
        functools.partial(_hy_fwd_body, n, width),
        grid=(b, n // tm, n // tk),
        in_specs=[mat, mat, pl.BlockSpec((None, tk, width), lambda bi, i, k: (bi, k, 0)), spec, spec],
        out_specs=pl.BlockSpec((None, tm, 2 * width), lambda bi, i, k: (bi, i, 0)),
        out_shape=jax.ShapeDtypeStruct((b, n, 2 * width), BF16),
        scratch_shapes=[pltpu.VMEM((tm, width), F32), pltpu.VMEM((tm, width), F32)],
        compiler_params=_params("parallel", "parallel", "arbitrary"),
        name="hyena_fwd_dft",
    )(cq, sq, vv, tre, tim)


def _hy_inv_body(n, width, c_ref, s_ref, p_ref, x0_ref, vv_ref, vn_ref, tn_ref, bias_ref, o_ref, acc_ref):
    k = pl.program_id(2)

    @pl.when(k == 0)
    def _():
        acc_ref[...] = jnp.zeros_like(acc_ref)

    acc_ref[...] += _dot(c_ref[...], p_ref[:, :width]) - _dot(s_ref[...], p_ref[:, width:])

    @pl.when(k == pl.num_programs(2) - 1)
    def _():
        tm = acc_ref.shape[0]
        pos = lax.broadcasted_iota(jnp.int32, (tm, 1), 0) + pl.program_id(1) * tm
        sign = jnp.where(pos % 2 == 0, 1.0, -1.0)
        nyq = (vn_ref[...] * tn_ref[...]) * (0.5 / n)
        conv = acc_ref[...] + sign * nyq
        o_ref[...] = (x0_ref[...] * (conv + vv_ref[...] * bias_ref[...])).astype(BF16)


def _hy_inv_call(cq, sq, p, x0, vv, vn, tn, bias):
    b, n, width = vv.shape
    tm = _tile(n, 512)
    tk = _tile(n, 512)
    mat = pl.BlockSpec((tm, tk), lambda bi, i, k: (i, k))
    tok = pl.BlockSpec((None, tm, width), lambda bi, i, k: (bi, i, 0))
    vec = pl.BlockSpec((1, width), lambda bi, i, k: (0, 0))
    return pl.pallas_call(
        functools.partial(_hy_inv_body, n, width),
        grid=(b, n // tm, n // tk),
        in_specs=[mat, mat, pl.BlockSpec((None, tk, 2 * width), lambda bi, i, k: (bi, k, 0)),
                  tok, tok, pl.BlockSpec((None, 1, width), lambda bi, i, k: (bi, 0, 0)), vec, vec],
        out_specs=tok,
        out_shape=jax.ShapeDtypeStruct((b, n, width), BF16),
        scratch_shapes=[pltpu.VMEM((tm, width), F32)],
        compiler_params=_params("parallel", "parallel", "arbitrary"),
        name="hyena_inv_dft",
    )(cq, sq, p, x0, vv, vn, tn, bias)


def _ret_body(dk, nc, lgf_ref, lgb_ref, q_ref, k_ref, v_ref, g_ref, gn_ref, s0f_ref, s0b_ref,
              y_ref, sfo_ref, sbo_ref, o_ref, sf_ref, sb_ref):
    h = pl.program_id(1)
    ch = RET_CHUNK
    sf_ref[...] = s0f_ref[...]
    sb_ref[...] = s0b_ref[...]

    ii = lax.broadcasted_iota(jnp.int32, (ch, ch), 0)
    jj = lax.broadcasted_iota(jnp.int32, (ch, ch), 1)
    diff = (ii - jj).astype(F32)
    idx = lax.broadcasted_iota(jnp.int32, (ch, 1), 0).astype(F32)
    k_scale = dk ** -0.5

    def decays(lg, lag, q_pow, k_pow):
        inner = jnp.where(lag >= 0, jnp.exp(lg * jnp.maximum(lag, 0.0)), 0.0)
        return inner, jnp.exp(lg * q_pow), jnp.exp(lg * k_pow), jnp.exp(lg * jnp.full((1, 1), float(ch), F32))

    dec_f = decays(lgf_ref[h], diff, idx + 1.0, (ch - 1.0) - idx)
    dec_b = decays(lgb_ref[h], -diff, float(ch) - idx, idx)

    def rows(c):
        return pl.ds(pl.multiple_of(c * ch, ch), ch)

    def chunk(c, dec, s_ref):
        inner_decay, q_decay, k_decay, chunk_decay = dec
        q = q_ref[rows(c), :]
        k = k_ref[rows(c), :].astype(F32) * k_scale
        v = v_ref[rows(c), :]
        scores = lax.dot_general(q, k.astype(BF16), (((1,), (1,)), ((), ())),
                                 preferred_element_type=F32) * inner_decay
        s = s_ref[...]
        out = _dot(scores.astype(BF16), v) + _dot(q, s.astype(BF16)) * q_decay
        kd = (k * k_decay).T.astype(BF16)
        s_ref[...] = s * chunk_decay + _dot(kd, v)
        return out

    def finish(c, o):
        mu = jnp.mean(o, axis=-1, keepdims=True)
        var = jnp.mean(jnp.square(o - mu), axis=-1, keepdims=True)
        on = (o - mu) * lax.rsqrt(var + EPS) * gn_ref[...]
        y_ref[rows(c), :] = (_silu(g_ref[rows(c), :].astype(F32)) * on).astype(BF16)

    def first_half(c, carry):
        cb = nc - 1 - c
        o_ref[rows(c), :] = chunk(c, dec_f, sf_ref)
        o_ref[rows(cb), :] = chunk(cb, dec_b, sb_ref)
        return carry

    def second_half(c, carry):
        cb = nc - 1 - c
        finish(c, o_ref[rows(c), :] + chunk(c, dec_f, sf_ref))
        finish(cb, o_ref[rows(cb), :] + chunk(cb, dec_b, sb_ref))
        return carry

    lax.fori_loop(0, nc // 2, first_half, 0)
    lax.fori_loop(nc // 2, nc, second_half, 0)
    sfo_ref[...] = sf_ref[...]
    sbo_ref[...] = sb_ref[...]


def _ret_call(u3, width, lgf, lgb, gn, s0f, s0b):
    b, n, _ = u3.shape
    dk = width // N_GROUPS
    nc = n // RET_CHUNK
    assert nc % 2 == 0

    def tok(part):
        return pl.BlockSpec((None, n, dk), lambda bi, h: (bi, 0, part * N_GROUPS + h))

    state = pl.BlockSpec((None, None, dk, dk), lambda bi, h: (bi, h, 0, 0))
    smem = pl.BlockSpec(memory_space=pltpu.SMEM)
    return pl.pallas_call(
        functools.partial(_ret_body, dk, nc),
        grid=(b, N_GROUPS),
        in_specs=[smem, smem, tok(0), tok(1), tok(2), tok(3),
                  pl.BlockSpec((1, dk), lambda bi, h: (0, h)), state, state],
        out_specs=[pl.BlockSpec((None, n, dk), lambda bi, h: (bi, 0, h)), state, state],
        out_shape=[jax.ShapeDtypeStruct((b, n, width), BF16),
                   jax.ShapeDtypeStruct((b, N_GROUPS, dk, dk), F32),
                   jax.ShapeDtypeStruct((b, N_GROUPS, dk, dk), F32)],
        scratch_shapes=[pltpu.VMEM((n, dk), F32), pltpu.VMEM((dk, dk), F32), pltpu.VMEM((dk, dk), F32)],
        compiler_params=_params("parallel", "parallel"),
        name="retention",
    )(lgf, lgb, u3, u3, u3, u3, gn, s0f, s0b)


def _pool_body(n, x_ref, w_ref, sc_ref, y_ref):
    g = pl.program_id(1)
    gc = x_ref.shape[1]
    n_pad = n + 2 * POOL_PAD
    pos = lax.broadcasted_iota(jnp.int32, (n, 1), 0)

    for gi, win in enumerate(POOL_WINDOWS):
        @pl.when(g == gi)
        def _(win=win):
            x = x_ref[...].astype(F32)
            zeros = jnp.zeros((POOL_PAD, gc), F32)
            s = jnp.concatenate([zeros, x, zeros], axis=0)
            span = 1
            while span < win:
                s = s + pltpu.roll(s, span, 0)
                span *= 2
            total = pltpu.roll(s, n_pad - (POOL_PAD + win // 2 - 1), 0)[:n]
            lo = jnp.clip(pos - win // 2, 0, n - 1)
            hi = jnp.clip(pos - win // 2 + win - 1, 0, n - 1)
            count = (hi - lo + 1).astype(F32)
            p = total / count - x
            y = _dot(p.astype(BF16), w_ref[...]) * sc_ref[...]
            y_ref[...] = y.astype(BF16)


def _pool_call(u3, col0, width, w, scale):
    b, n, _ = u3.shape
    gc = width // N_GROUPS
    assert max(POOL_WINDOWS) // 2 <= POOL_PAD
    return pl.pallas_call(
        functools.partial(_pool_body, n),
        grid=(b, N_GROUPS),
        in_specs=[pl.BlockSpec((None, n, gc), lambda bi, g: (bi, 0, col0 // gc + g)),
                  pl.BlockSpec((None, gc, gc), lambda bi, g: (g, 0, 0)),
                  pl.BlockSpec((1, gc), lambda bi, g: (0, g))],
        out_specs=pl.BlockSpec((None, n, gc), lambda bi, g: (bi, 0, g)),
        out_shape=jax.ShapeDtypeStruct((b, n, width), BF16),
        compiler_params=_params("parallel", "parallel"),
        name="pool_mix",
    )(u3, w, scale)


def _add_pos_body(x_ref, p_ref, o_ref):
    o_ref[...] = x_ref[...] + p_ref[...]


def _add_pos_call(x3, pos):
    b, n, d = x3.shape
    tr = _tile(n, 512)
    tok = pl.BlockSpec((None, tr, d), lambda bi, i: (bi, i, 0))
    return pl.pallas_call(
        _add_pos_body,
        grid=(b, n // tr),
        in_specs=[tok, pl.BlockSpec((tr, d), lambda bi, i: (i, 0))],
        out_specs=tok,
        out_shape=jax.ShapeDtypeStruct((b, n, d), F32),
        compiler_params=_params("parallel", "parallel"),
        name="add_pos_embed",
    )(x3, pos)


def _dft_tables(n, period):
    k = jnp.arange(n, dtype=jnp.int32)
    m = (k[:, None] * k[None, :]) % period
    ang = m.astype(F32) * (2.0 * math.pi / period)
    return jnp.cos(ang).astype(BF16), jnp.sin(ang).astype(BF16)


def _fnet_tables(cq, sq):
    n = cq.shape[0]
    sign = jnp.where(jnp.arange(n) % 2 == 0, 1.0, -1.0).astype(BF16)[None, :]
    ce, se = cq[0::2], sq[0::2]
    return jnp.concatenate([ce, ce * sign], axis=0), jnp.concatenate([se, se * sign], axis=0)


def _grid_pos_embed(n_tokens, d):
    rows = n_tokens // GRID_W
    rr, cc = jnp.meshgrid(jnp.arange(rows, dtype=F32), jnp.arange(GRID_W, dtype=F32), indexing='ij')
    rr = rr.reshape(-1)[:, None]
    cc = cc.reshape(-1)[:, None]
    quarter = d // 4
    omega = 1.0 / (POS_BASE ** (jnp.arange(quarter, dtype=F32) / quarter))
    ar, ac = rr * omega, cc * omega
    return jnp.concatenate([jnp.sin(ar), jnp.cos(ar), jnp.sin(ac), jnp.cos(ac)], axis=-1)


def _hyena_pos_features(n):
    t_idx = jnp.arange(n, dtype=F32)
    t = jnp.linspace(0.0, 1.0, n, dtype=F32)
    bands = jnp.linspace(1e-4, HY_BANDS - 1, HY_BANDS, dtype=F32)
    ang = (2.0 * math.pi / n) * t_idx[:, None] * bands[None, :]
    z = jnp.concatenate([t[:, None], jnp.cos(ang), jnp.sin(ang)], axis=-1)
    return jnp.pad(z, ((0, 0), (0, LANE - z.shape[1])))


def _pad2(a, rows, cols):
    return jnp.pad(a, ((0, rows - a.shape[0]), (0, cols - a.shape[1])))


def kernel(x_prompt, x_sample, state_ret_fwd, state_ret_bwd, c, c_ctx, norm_g, mod_w, mod_b, ffn_a_in, ffn_a_out, ffn_b_in, ffn_b_out, ev_in_w, ev_out_w, fnet_w, hy_conv_w, hy_conv_b, hy_w1, hy_b1, hy_w2, hy_b2, hy_w_out, hy_freq, hy_decay, hy_bias, od_in_w, od_out_w, ret_log_decay_fwd, ret_log_decay_bwd, ret_gn, pool_w, pool_scale, final_norm):
    depth, d = norm_g.shape[0], norm_g.shape[2]
    half = d // 2
    dec_b = x_sample.shape[0]
    assert dec_b + 1 <= MOD_ROWS

    cond = jnp.concatenate([c, c_ctx[None, :]], axis=0)
    cond = jnp.pad(cond, ((0, MOD_ROWS - cond.shape[0]), (0, 0)))
    mod = _mod_call(cond, mod_w, mod_b).reshape(depth, MOD_ROWS, N_MOD, 1, d)

    bf = lambda a: a.astype(BF16)
    ffn_a_in, ffn_a_out, ffn_b_in, ffn_b_out = bf(ffn_a_in), bf(ffn_a_out), bf(ffn_b_in), bf(ffn_b_out)
    ev_in_w, ev_out_w, od_in_w, od_out_w = bf(ev_in_w), bf(ev_out_w), bf(od_in_w), bf(od_out_w)
    fnet_w, pool_w = bf(fnet_w), bf(pool_w)

    hid = hy_w1.shape[2]
    n_even = hy_w1.shape[0]
    hy_w1p = jnp.stack([_pad2(hy_w1[e], LANE, LANE) for e in range(n_even)])
    hy_w2p = jnp.stack([_pad2(hy_w2[e], LANE, LANE) for e in range(n_even)])
    hy_woutp = jnp.pad(hy_w_out, ((0, 0), (0, LANE - hid), (0, 0)))
    padv = lambda a: jnp.pad(a, ((0, 0), (0, LANE - hid)))[:, None, :]
    hy_b1p, hy_b2p, hy_freqp = padv(hy_b1), padv(hy_b2), padv(hy_freq)

    gc = half // N_GROUPS
    cs_chan = jnp.concatenate(_dft_tables(gc, gc), axis=1)

    def trunk(x3, mod_rows, s0_f, s0_b):
        b, n, _ = x3.shape
        t = b * n
        rows_per_mod = n if mod_rows.stop - mod_rows.start > 1 else t
        x = x3.reshape(t, d)
        cq, sq = _dft_tables(n, 2 * n)
        cn, sn = _fnet_tables(cq, sq)
        z_pos = _hyena_pos_features(n)
        s_f_out, s_b_out = [], []
        for l in range(depth):
            m = [mod[l, mod_rows, i] for i in range(N_MOD)]
            x = _ffn_call(x, norm_g[l, 0][None], m[0], m[1], m[2], ffn_a_in, ffn_a_out, l, rows_per_mod)
            if l % 2 == 0:
                e = l // 2
                u3 = _inproj_call(x, norm_g[l, 1][None], m[3], m[4], ev_in_w, e, rows_per_mod).reshape(b, n, -1)
                zc = _fnet_chan_call(u3, cs_chan, half)
                ya = _fnet_seq_call(cn, sn, zc, fnet_w[e])
                x0, vv, vn = _hy_pre_call(u3, half, half, hy_conv_w[e], hy_conv_b[e][None])
                hs, hd, tn = _hy_filter_call(z_pos, hy_w1p[e], hy_b1p[e], hy_w2p[e], hy_b2p[e], hy_freqp[e],
                                             hy_woutp[e], hy_decay[e][None])
                tre, tim = _hy_spec_call(cq, sq, hs, hd)
                p = _hy_fwd_call(cq, sq, vv, tre, tim)
                yb = _hy_inv_call(cq, sq, p, x0, vv, vn, tn, hy_bias[e][None])
                w_out, w_idx = ev_out_w, e
            else:
                o = l // 2
                u3 = _inproj_call(x, norm_g[l, 1][None], m[3], m[4], od_in_w, o, rows_per_mod).reshape(b, n, -1)
                ya, s_f, s_b = _ret_call(u3, half, ret_log_decay_fwd[o], ret_log_decay_bwd[o], ret_gn[o][None],
                                         s0_f[:, o], s0_b[:, o])
                yb = _pool_call(u3, 4 * half, half, pool_w[o], pool_scale[o][None])
                s_f_out.append(s_f)
                s_b_out.append(s_b)
                w_out, w_idx = od_out_w, o
            x = _outproj_call(x, ya.reshape(t, half), yb.reshape(t, half), w_out, w_idx, m[5], rows_per_mod)
            x = _ffn_call(x, norm_g[l, 2][None], m[6], m[7], m[8], ffn_b_in, ffn_b_out, l, rows_per_mod,
                          final_g=final_norm[None] if l == depth - 1 else None)
        return x.reshape(b, n, d), jnp.stack(s_f_out, axis=1), jnp.stack(s_b_out, axis=1)

    zero_state = jnp.zeros((x_prompt.shape[0],) + state_ret_fwd.shape[1:], F32)
    y_prompt, st_f, st_b = trunk(x_prompt, slice(dec_b, dec_b + 1), zero_state, zero_state)

    x_lat = _add_pos_call(x_sample, _grid_pos_embed(x_sample.shape[1], d))
    y_sample, _, _ = trunk(x_lat, slice(0, dec_b), state_ret_fwd, state_ret_bwd)
    return (y_prompt, y_sample, st_f, st_b)
```

```python
import functools
import math

import jax
import jax.numpy as jnp
from jax import lax
from jax.experimental import pallas as pl
from jax.experimental.pallas import tpu as pltpu

F32 = jnp.float32
BF16 = jnp.bfloat16

EPS = 1e-6
N_MOD = 9
GRID_W = 64
POS_BASE = 10000.0
N_GROUPS = 4
HY_BANDS = 16
RET_CHUNK = 128
POOL_WINDOWS = (2, 4, 8, 16)
POOL_PAD = 8
LANE = 128
MOD_ROWS = 16
VMEM_LIMIT = 56 * 1024 * 1024
NORM_ROWS = 32
NORM_UNROLL = 4


def _params(*sem):
    return pltpu.CompilerParams(dimension_semantics=sem, vmem_limit_bytes=VMEM_LIMIT)


def _tile(n, pref):
    if n <= pref:
        return n
    t = pref
    while n % t:
        t -= LANE
    assert t > 0, (n, pref)
    return t


def _silu(x):
    return x * (1.0 / (1.0 + jnp.exp(-x)))


def _norm_mod_into(h_ref, x_ref, g_ref, shift_ref, scale_ref):
    gs = g_ref[...] * (1.0 + scale_ref[...])
    shift = shift_ref[...]

    def body(r, carry):
        rows = pl.ds(pl.multiple_of(r * NORM_ROWS, NORM_ROWS), NORM_ROWS)
        x = x_ref[rows, :]
        ms = jnp.mean(x * x, axis=-1, keepdims=True)
        h_ref[rows, :] = (x * lax.rsqrt(ms + EPS) * gs + shift).astype(h_ref.dtype)
        return carry

    lax.fori_loop(0, x_ref.shape[0] // NORM_ROWS, body, 0, unroll=NORM_UNROLL)


def _dot(a, b):
    return jnp.dot(a, b, preferred_element_type=F32)


def _mod_body(c_ref, w_ref, b_ref, o_ref):
    s = _silu(c_ref[...]).astype(BF16)
    o_ref[...] = _dot(s, w_ref[...].astype(BF16)) + b_ref[...]


def _mod_call(cond, mod_w, mod_b):
    depth, d, n_out = mod_w.shape
    tn = _tile(n_out, 1024)
    return pl.pallas_call(
        _mod_body,
        grid=(depth, n_out // tn),
        in_specs=[
            pl.BlockSpec((MOD_ROWS, d), lambda l, j: (0, 0)),
            pl.BlockSpec((None, d, tn), lambda l, j: (l, 0, j)),
            pl.BlockSpec((None, 1, tn), lambda l, j: (l, 0, j)),
        ],
        out_specs=pl.BlockSpec((None, MOD_ROWS, tn), lambda l, j: (l, 0, j)),
        out_shape=jax.ShapeDtypeStruct((depth, MOD_ROWS, n_out), F32),
        compiler_params=_params("parallel", "parallel"),
        name="mod_proj",
    )(cond, mod_w, mod_b.reshape(depth, 1, n_out))


def _ffn_body(final, with_pos, x_ref, g_ref, sh_ref, sc_ref, gt_ref, wg_ref, wu_ref, wo_ref, *rest):
    rest = list(rest)
    pos_ref = rest.pop(0) if with_pos else None
    fg_ref = rest.pop(0) if final else None
    o_ref, h_ref, acc_ref = rest
    j = pl.program_id(1)
    res_ref = o_ref if with_pos else x_ref

    @pl.when(j == 0)
    def _():
        if with_pos:
            o_ref[...] = x_ref[...] + pos_ref[...]
        _norm_mod_into(h_ref, res_ref, g_ref, sh_ref, sc_ref)
        acc_ref[...] = jnp.zeros_like(acc_ref)

    h = h_ref[...]
    gate = _dot(h, wg_ref[...])
    up = _dot(h, wu_ref[...])
    a = (_silu(gate) * up).astype(BF16)
    acc_ref[...] += _dot(a, wo_ref[...])

    @pl.when(j == pl.num_programs(1) - 1)
    def _():
        y = res_ref[...] + 0.5 * gt_ref[...] * acc_ref[...]
        if final:
            ms = jnp.mean(y * y, axis=-1, keepdims=True)
            y = y * lax.rsqrt(ms + EPS) * fg_ref[...]
        o_ref[...] = y


def _ffn_call(x, g, shift, scale, gate, w_in, w_out, layer, rows_per_mod, final_g=None, pos=None):
    t, d = x.shape
    d_ff = w_out.shape[1]
    tm = _tile(rows_per_mod, 512)
    tf = _tile(d_ff, 512)
    nf = d_ff // tf
    final = final_g is not None
    with_pos = pos is not None
    tiles_per_mod = rows_per_mod // tm
    mod_spec = pl.BlockSpec((None, 1, d), lambda i, j: (i // tiles_per_mod, 0, 0))
    vec_spec = pl.BlockSpec((1, d), lambda i, j: (0, 0))
    in_specs = [
        pl.BlockSpec((tm, d), lambda i, j: (i, 0)),
        vec_spec, mod_spec, mod_spec, mod_spec,
        pl.BlockSpec((None, d, tf), lambda i, j: (layer, 0, j)),
        pl.BlockSpec((None, d, tf), lambda i, j: (layer, 0, j + nf)),
        pl.BlockSpec((None, tf, d), lambda i, j: (layer, j, 0)),
    ]
    args = [x, g, shift, scale, gate, w_in, w_in, w_out]
    if with_pos:
        in_specs.append(pl.BlockSpec((tm, d), lambda i, j: (i % tiles_per_mod, 0)))
        args.append(pos)
    if final:
        in_specs.append(vec_spec)
        args.append(final_g)
    return pl.pallas_call(
        functools.partial(_ffn_body, final, with_pos),
        grid=(t // tm, nf),
        in_specs=in_specs,
        out_specs=pl.BlockSpec((tm, d), lambda i, j: (i, 0)),
        out_shape=jax.ShapeDtypeStruct((t, d), F32),
        scratch_shapes=[pltpu.VMEM((tm, d), BF16), pltpu.VMEM((tm, d), F32)],
        compiler_params=_params("parallel", "arbitrary"),
        name="ffn",
    )(*args)


def _inproj_body(x_ref, g_ref, sh_ref, sc_ref, w_ref, o_ref, h_ref):
    @pl.when(pl.program_id(1) == 0)
    def _():
        _norm_mod_into(h_ref, x_ref, g_ref, sh_ref, sc_ref)

    o_ref[...] = _dot(h_ref[...], w_ref[...]).astype(BF16)


def _inproj_call(x, g, shift, scale, w, layer, rows_per_mod):
    t, d = x.shape
    n_out = w.shape[2]
    tm = _tile(rows_per_mod, 1024)
    tn = _tile(n_out, 512)
    mod_spec = pl.BlockSpec((None, 1, d), lambda i, j: ((i * tm) // rows_per_mod, 0, 0))
    return pl.pallas_call(
        _inproj_body,
        grid=(t // tm, n_out // tn),
        in_specs=[
            pl.BlockSpec((tm, d), lambda i, j: (i, 0)),
            pl.BlockSpec((1, d), lambda i, j: (0, 0)),
            mod_spec, mod_spec,
            pl.BlockSpec((None, d, tn), lambda i, j: (layer, 0, j)),
        ],
        out_specs=pl.BlockSpec((tm, tn), lambda i, j: (i, j)),
        out_shape=jax.ShapeDtypeStruct((t, n_out), BF16),
        scratch_shapes=[pltpu.VMEM((tm, d), BF16)],
        compiler_params=_params("parallel", "arbitrary"),
        name="mix_in_proj",
    )(x, g, shift, scale, w)


def _outproj_body(x_ref, ya_ref, yb_ref, wa_ref, wb_ref, gt_ref, o_ref):
    y = _dot(ya_ref[...], wa_ref[...]) + _dot(yb_ref[...], wb_ref[...])
    o_ref[...] = x_ref[...] + gt_ref[...] * y


def _outproj_call(x, ya, yb, w, layer, gate, rows_per_mod):
    t, d = x.shape
    half = ya.shape[1]
    tm = _tile(rows_per_mod, 512)
    return pl.pallas_call(
        _outproj_body,
        grid=(t // tm,),
        in_specs=[
            pl.BlockSpec((tm, d), lambda i: (i, 0)),
            pl.BlockSpec((tm, half), lambda i: (i, 0)),
            pl.BlockSpec((tm, half), lambda i: (i, 0)),
            pl.BlockSpec((None, half, d), lambda i: (layer, 0, 0)),
            pl.BlockSpec((None, half, d), lambda i: (layer, 1, 0)),
            pl.BlockSpec((None, 1, d), lambda i: ((i * tm) // rows_per_mod, 0, 0)),
        ],
        out_specs=pl.BlockSpec((tm, d), lambda i: (i, 0)),
        out_shape=jax.ShapeDtypeStruct((t, d), F32),
        compiler_params=_params("parallel"),
        name="mix_out_proj",
    )(x, ya, yb, w, w, gate)


def _fnet_chan_body(gc, x_ref, cs_ref, z_ref):
    half = N_GROUPS * gc
    for g in range(N_GROUPS):
        xg = x_ref[:, g * gc:(g + 1) * gc].astype(BF16)
        z = _dot(xg, cs_ref[...])
        z_ref[:, g * gc:(g + 1) * gc] = z[:, :gc].astype(BF16)
        z_ref[:, half + g * gc:half + (g + 1) * gc] = z[:, gc:].astype(BF16)


def _fnet_chan_call(u3, cs, width):
    b, n, _ = u3.shape
    gc = width // N_GROUPS
    tr = _tile(n, 512)
    return pl.pallas_call(
        functools.partial(_fnet_chan_body, gc),
        grid=(b, n // tr),
        in_specs=[
            pl.BlockSpec((None, tr, width), lambda bi, i: (bi, i, 0)),
            pl.BlockSpec((gc, 2 * gc), lambda bi, i: (0, 0)),
        ],
        out_specs=pl.BlockSpec((None, tr, 2 * width), lambda bi, i: (bi, i, 0)),
        out_shape=jax.ShapeDtypeStruct((b, n, 2 * width), BF16),
        compiler_params=_params("parallel", "parallel"),
        name="fnet_chan_dft",
    )(u3, cs)


def _fnet_seq_body(gc, scale, c_ref, s_ref, z_ref, w_ref, o_ref, acc_ref):
    k = pl.program_id(2)
    half = N_GROUPS * gc

    @pl.when(k == 0)
    def _():
        acc_ref[...] = jnp.zeros_like(acc_ref)

    acc_ref[...] += _dot(c_ref[...], z_ref[:, :half]) - _dot(s_ref[...], z_ref[:, half:])

    @pl.when(k == pl.num_programs(2) - 1)
    def _():
        for g in range(N_GROUPS):
            r = (acc_ref[:, g * gc:(g + 1) * gc] * scale).astype(BF16)
            o_ref[:, g * gc:(g + 1) * gc] = _dot(r, w_ref[g]).astype(BF16)


def _fnet_seq_call(cn, sn, z, w):
    b, n, two_w = z.shape
    width = two_w // 2
    gc = width // N_GROUPS
    tm = _tile(n, 1024)
    tk = _tile(n, 512)
    scale = 1.0 / math.sqrt(n * gc)
    return pl.pallas_call(
        functools.partial(_fnet_seq_body, gc, scale),
        grid=(b, n // tm, n // tk),
        in_specs=[
            pl.BlockSpec((tm, tk), lambda bi, i, k: (i, k)),
            pl.BlockSpec((tm, tk), lambda bi, i, k: (i, k)),
            pl.BlockSpec((None, tk, two_w), lambda bi, i, k: (bi, k, 0)),
            pl.BlockSpec((N_GROUPS, gc, gc), lambda bi, i, k: (0, 0, 0)),
        ],
        out_specs=pl.BlockSpec((None, tm, width), lambda bi, i, k: (bi, i, 0)),
        out_shape=jax.ShapeDtypeStruct((b, n, width), BF16),
        scratch_shapes=[pltpu.VMEM((tm, width), F32)],
        compiler_params=_params("parallel", "parallel", "arbitrary"),
        name="fnet_seq_dft",
    )(cn, sn, z, w)


def _hy_pre_body(n, u0_ref, u1_ref, u2_ref, w0_ref, w1_ref, w2_ref, b0_ref, b1_ref, b2_ref,
                 x0_ref, vv_ref, vn_ref):
    row = lax.broadcasted_iota(jnp.int32, u0_ref.shape, 0)

    def conv(u_ref, w_ref, b_ref):
        u = u_ref[...].astype(F32)
        prev = jnp.where(row == 0, 0.0, pltpu.roll(u, 1, 0))
        nxt = jnp.where(row == n - 1, 0.0, pltpu.roll(u, n - 1, 0))
        return prev * w_ref[0:1, :] + u * w_ref[1:2, :] + nxt * w_ref[2:3, :] + b_ref[...]

    x0_ref[...] = conv(u0_ref, w0_ref, b0_ref).astype(BF16)
    vv = conv(u2_ref, w2_ref, b2_ref) * conv(u1_ref, w1_ref, b1_ref)
    vv_ref[...] = vv.astype(BF16)
    vn_ref[...] = jnp.sum(jnp.where(row % 2 == 0, vv, -vv), axis=0, keepdims=True)


def _hy_pre_call(u3, col0, width, conv_w, conv_b):
    b, n, _ = u3.shape
    tc = _tile(width, LANE)
    nct = width // tc

    def u_spec(s):
        return pl.BlockSpec((None, n, tc), lambda bi, c: (bi, 0, (col0 + s * width) // tc + c))

    def w_spec(s):
        return pl.BlockSpec((3, tc), lambda bi, c: (0, s * nct + c))

    def b_spec(s):
        return pl.BlockSpec((1, tc), lambda bi, c: (0, s * nct + c))

    out3 = pl.BlockSpec((None, n, tc), lambda bi, c: (bi, 0, c))
    return pl.pallas_call(
        functools.partial(_hy_pre_body, n),
        grid=(b, nct),
        in_specs=[u_spec(0), u_spec(1), u_spec(2), w_spec(0), w_spec(1), w_spec(2),
                  b_spec(0), b_spec(1), b_spec(2)],
        out_specs=[out3, out3, pl.BlockSpec((None, 1, tc), lambda bi, c: (bi, 0, c))],
        out_shape=[jax.ShapeDtypeStruct((b, n, width), BF16),
                   jax.ShapeDtypeStruct((b, n, width), BF16),
                   jax.ShapeDtypeStruct((b, 1, width), F32)],
        compiler_params=_params("parallel", "parallel"),
        name="hyena_short_conv",
    )(u3, u3, u3, conv_w, conv_w, conv_w, conv_b, conv_b, conv_b)


def _hy_filter_body(z_ref, w1_ref, b1_ref, w2_ref, b2_ref, fr_ref, wf_ref, wb_ref, dec_ref,
                    hs_ref, hd_ref, tn_ref):
    hp = lax.Precision.HIGHEST
    z = z_ref[...]
    fr = fr_ref[...]
    h = jnp.sin(fr * (jnp.dot(z, w1_ref[...], precision=hp, preferred_element_type=F32) + b1_ref[...]))
    h = jnp.sin(fr * (jnp.dot(h, w2_ref[...], precision=hp, preferred_element_type=F32) + b2_ref[...]))
    window = jnp.exp(-z[:, 0:1] * jnp.abs(dec_ref[...]))
    hf = jnp.dot(h, wf_ref[...], precision=hp, preferred_element_type=F32) * window
    hb = jnp.dot(h, wb_ref[...], precision=hp, preferred_element_type=F32) * window
    row = lax.broadcasted_iota(jnp.int32, hf.shape, 0)
    hb = jnp.where(row == 0, 0.0, hb)
    l1 = jnp.sum(jnp.abs(hf), axis=0, keepdims=True) + jnp.sum(jnp.abs(hb), axis=0, keepdims=True)
    hs = (hf + hb) / l1
    hd = (hb - hf) / l1
    hs_ref[...] = hs.astype(BF16)
    hd_ref[...] = hd.astype(BF16)
    tn_ref[...] = jnp.sum(jnp.where(row % 2 == 0, hs, -hs), axis=0, keepdims=True)


def _hy_filter_call(z, w1, b1, w2, b2, freq, w_out, decay):
    n, kz = z.shape
    hid = w1.shape[1]
    width = decay.shape[1]
    tc = _tile(width, 256)
    nct = width // tc
    full = lambda a: pl.BlockSpec(a.shape, lambda c: (0, 0))
    col = pl.BlockSpec((n, tc), lambda c: (0, c))
    vec = pl.BlockSpec((1, tc), lambda c: (0, c))
    return pl.pallas_call(
        _hy_filter_body,
        grid=(nct,),
        in_specs=[full(z), full(w1), full(b1), full(w2), full(b2), full(freq),
                  pl.BlockSpec((hid, tc), lambda c: (0, c)),
                  pl.BlockSpec((hid, tc), lambda c: (0, c + nct)),
                  vec],
        out_specs=[col, col, vec],
        out_shape=[jax.ShapeDtypeStruct((n, width), BF16),
                   jax.ShapeDtypeStruct((n, width), BF16),
                   jax.ShapeDtypeStruct((1, width), F32)],
        compiler_params=_params("parallel"),
        name="hyena_filter",
    )(z, w1, b1, w2, b2, freq, w_out, w_out, decay)


def _hy_spec_body(c_ref, s_ref, hs_ref, hd_ref, tre_ref, tim_ref):
    @pl.when(pl.program_id(1) == 0)
    def _():
        tre_ref[...] = jnp.zeros_like(tre_ref)
        tim_ref[...] = jnp.zeros_like(tim_ref)

    tre_ref[...] += _dot(c_ref[...], hs_ref[...])
    tim_ref[...] += _dot(s_ref[...], hd_ref[...])


def _hy_spec_call(cq, sq, hs, hd):
    n, width = hs.shape
    tm = _tile(n, 512)
    tk = _tile(n, 512)
    mat = pl.BlockSpec((tm, tk), lambda i, k: (i, k))
    rhs = pl.BlockSpec((tk, width), lambda i, k: (k, 0))
    out = pl.BlockSpec((tm, width), lambda i, k: (i, 0))
    return pl.pallas_call(
        _hy_spec_body,
        grid=(n // tm, n // tk),
        in_specs=[mat, mat, rhs, rhs],
        out_specs=[out, out],
        out_shape=[jax.ShapeDtypeStruct((n, width), F32)] * 2,
        compiler_params=_params("parallel", "arbitrary"),
        name="hyena_filter_spectrum",
    )(cq, sq, hs, hd)


def _hy_fwd_body(n, width, c_ref, s_ref, v_ref, tre_ref, tim_ref, p_ref, ar_ref, ai_ref):
    k = pl.program_id(2)

    @pl.when(k == 0)
    def _():
        ar_ref[...] = jnp.zeros_like(ar_ref)
        ai_ref[...] = jnp.zeros_like(ai_ref)

    v = v_ref[...]
    ar_ref[...] += _dot(c_ref[...], v)
    ai_ref[...] += _dot(s_ref[...], v)

    @pl.when(k == pl.num_programs(2) - 1)
    def _():
        tm = ar_ref.shape[0]
        freq = lax.broadcasted_iota(jnp.int32, (tm, 1), 0) + pl.program_id(1) * tm
        wk = jnp.where(freq == 0, 0.5 / n, 1.0 / n)
        ar, ai, tre, tim = ar_ref[...], ai_ref[...], tre_ref[...], tim_ref[...]
        p_ref[:, :width] = ((ar * tre + ai * tim) * wk).astype(BF16)
        p_ref[:, width:] = ((ar * tim - ai * tre) * wk).astype(BF16)


def _hy_fwd_call(cq, sq, vv, tre, tim):
    b, n, width = vv.shape
    tm = _tile(n, 1024)
    tk = _tile(n, 512)
    mat = pl.BlockSpec((tm, tk), lambda bi, i, k: (i, k))
    spec = pl.BlockSpec((tm, width), lambda bi, i, k: (i, 0))
    return pl.pallas_call(
        functools.partial(_hy_fwd_body, n, width),
        grid=(b, n // tm, n // tk),
        in_specs=[mat, mat, pl.BlockSpec((None, tk, width), lambda bi, i, k: (bi, k, 0)), spec, spec],
        out_specs=pl.BlockSpec((None, tm, 2 * width), lambda bi, i, k: (bi, i, 0)),
        out_shape=jax.ShapeDtypeStruct((b, n, 2 * width), BF16),
        scratch_shapes=[pltpu.VMEM((tm, width), F32), pltpu.VMEM((tm, width), F32)],
        compiler_params=_params("parallel", "parallel", "arbitrary"),
        name="hyena_fwd_dft",
    )(cq, sq, vv, tre, tim)


def _hy_inv_body(n, width, c_ref, s_ref, p_ref, x0_ref, vv_ref, vn_ref, tn_ref, bias_ref, o_ref, acc_ref):
    k = pl.program_id(2)

    @pl.when(k == 0)
    def _():
        acc_ref[...] = jnp.zeros_like(acc_ref)

    acc_ref[...] += _dot(c_ref[...], p_ref[:, :width]) - _dot(s_ref[...], p_ref[:, width:])

    @pl.when(k == pl.num_programs(2) - 1)
    def _():
        tm = acc_ref.shape[0]
        pos = lax.broadcasted_iota(jnp.int32, (tm, 1), 0) + pl.program_id(1) * tm
        sign = jnp.where(pos % 2 == 0, 1.0, -1.0)
        nyq = (vn_ref[...] * tn_ref[...]) * (0.5 / n)
        conv = acc_ref[...] + sign * nyq
        o_ref[...] = (x0_ref[...].astype(F32) * (conv + vv_ref[...].astype(F32) * bias_ref[...])).astype(BF16)


def _hy_inv_call(cq, sq, p, x0, vv, vn, tn, bias):
    b, n, width = vv.shape
    tm = _tile(n, 1024)
    tk = _tile(n, 512)
    mat = pl.BlockSpec((tm, tk), lambda bi, i, k: (i, k))
    tok = pl.BlockSpec((None, tm, width), lambda bi, i, k: (bi, i, 0))
    vec = pl.BlockSpec((1, width), lambda bi, i, k: (0, 0))
    return pl.pallas_call(
        functools.partial(_hy_inv_body, n, width),
        grid=(b, n // tm, n // tk),
        in_specs=[mat, mat, pl.BlockSpec((None, tk, 2 * width), lambda bi, i, k: (bi, k, 0)),
                  tok, tok, pl.BlockSpec((None, 1, width), lambda bi, i, k: (bi, 0, 0)), vec, vec],
        out_specs=tok,
        out_shape=jax.ShapeDtypeStruct((b, n, width), BF16),
        scratch_shapes=[pltpu.VMEM((tm, width), F32)],
        compiler_params=_params("parallel", "parallel", "arbitrary"),
        name="hyena_inv_dft",
    )(cq, sq, p, x0, vv, vn, tn, bias)


def _ret_body(dk, nc, lgf_ref, lgb_ref, q_ref, k_ref, v_ref, g_ref, gn_ref, s0f_ref, s0b_ref,
              y_ref, sfo_ref, sbo_ref, o_ref, sf_ref, sb_ref):
    h = pl.program_id(1)
    ch = RET_CHUNK
    sf_ref[...] = s0f_ref[...]
    sb_ref[...] = s0b_ref[...]

    ii = lax.broadcasted_iota(jnp.int32, (ch, ch), 0)
    jj = lax.broadcasted_iota(jnp.int32, (ch, ch), 1)
    diff = (ii - jj).astype(F32)
    idx = lax.broadcasted_iota(jnp.int32, (ch, 1), 0).astype(F32)
    k_scale = dk ** -0.5

    def decays(lg, lag, q_pow, k_pow):
        inner = jnp.where(lag >= 0, jnp.exp(lg * jnp.maximum(lag, 0.0)), 0.0)
        return inner, jnp.exp(lg * q_pow), jnp.exp(lg * k_pow), jnp.exp(lg * jnp.full((1, 1), float(ch), F32))

    dec_f = decays(lgf_ref[h], diff, idx + 1.0, (ch - 1.0) - idx)
    dec_b = decays(lgb_ref[h], -diff, float(ch) - idx, idx)

    def rows(c):
        return pl.ds(pl.multiple_of(c * ch, ch), ch)

    def chunk(c, dec, s_ref):
        inner_decay, q_decay, k_decay, chunk_decay = dec
        q = q_ref[rows(c), :]
        k = k_ref[rows(c), :].astype(F32) * k_scale
        v = v_ref[rows(c), :]
        scores = lax.dot_general(q, k.astype(BF16), (((1,), (1,)), ((), ())),
                                 preferred_element_type=F32) * inner_decay
        s = s_ref[...]
        out = _dot(scores.astype(BF16), v) + _dot(q, s.astype(BF16)) * q_decay
        kd = (k * k_decay).T.astype(BF16)
        s_ref[...] = s * chunk_decay + _dot(kd, v)
        return out

    def finish(c, o):
        mu = jnp.mean(o, axis=-1, keepdims=True)
        var = jnp.mean(jnp.square(o - mu), axis=-1, keepdims=True)
        on = (o - mu) * lax.rsqrt(var + EPS) * gn_ref[...]
        y_ref[rows(c), :] = (_silu(g_ref[rows(c), :].astype(F32)) * on).astype(BF16)

    def first_half(c, carry):
        cb = nc - 1 - c
        o_ref[rows(c), :] = chunk(c, dec_f, sf_ref)
        o_ref[rows(cb), :] = chunk(cb, dec_b, sb_ref)
        return carry

    def second_half(c, carry):
        cb = nc - 1 - c
        finish(c, o_ref[rows(c), :] + chunk(c, dec_f, sf_ref))
        finish(cb, o_ref[rows(cb), :] + chunk(cb, dec_b, sb_ref))
        return carry

    lax.fori_loop(0, nc // 2, first_half, 0)
    lax.fori_loop(nc // 2, nc, second_half, 0)
    sfo_ref[...] = sf_ref[...]
    sbo_ref[...] = sb_ref[...]


def _ret_call(u3, width, lgf, lgb, gn, s0f, s0b):
    b, n, _ = u3.shape
    dk = width // N_GROUPS
    nc = n // RET_CHUNK
    assert nc % 2 == 0

    def tok(part):
        return pl.BlockSpec((None, n, dk), lambda bi, h: (bi, 0, part * N_GROUPS + h))

    state = pl.BlockSpec((None, None, dk, dk), lambda bi, h: (bi, h, 0, 0))
    smem = pl.BlockSpec(memory_space=pltpu.SMEM)
    return pl.pallas_call(
        functools.partial(_ret_body, dk, nc),
        grid=(b, N_GROUPS),
        in_specs=[smem, smem, tok(0), tok(1), tok(2), tok(3),
                  pl.BlockSpec((1, dk), lambda bi, h: (0, h)), state, state],
        out_specs=[pl.BlockSpec((None, n, dk), lambda bi, h: (bi, 0, h)), state, state],
        out_shape=[jax.ShapeDtypeStruct((b, n, width), BF16),
                   jax.ShapeDtypeStruct((b, N_GROUPS, dk, dk), F32),
                   jax.ShapeDtypeStruct((b, N_GROUPS, dk, dk), F32)],
        scratch_shapes=[pltpu.VMEM((n, dk), F32), pltpu.VMEM((dk, dk), F32), pltpu.VMEM((dk, dk), F32)],
        compiler_params=_params("parallel", "parallel"),
        name="retention",
    )(lgf, lgb, u3, u3, u3, u3, gn, s0f, s0b)


def _pool_body(n, x_ref, w_ref, sc_ref, y_ref):
    g = pl.program_id(1)
    gc = x_ref.shape[1]
    n_pad = n + 2 * POOL_PAD
    pos = lax.broadcasted_iota(jnp.int32, (n, 1), 0)

    for gi, win in enumerate(POOL_WINDOWS):
        @pl.when(g == gi)
        def _(win=win):
            x = x_ref[...].astype(F32)
            zeros = jnp.zeros((POOL_PAD, gc), F32)
            s = jnp.concatenate([zeros, x, zeros], axis=0)
            span = 1
            while span < win:
                s = s + pltpu.roll(s, span, 0)
                span *= 2
            total = pltpu.roll(s, n_pad - (POOL_PAD + win // 2 - 1), 0)[:n]
            lo = jnp.clip(pos - win // 2, 0, n - 1)
            hi = jnp.clip(pos - win // 2 + win - 1, 0, n - 1)
            count = (hi - lo + 1).astype(F32)
            p = total / count - x
            y = _dot(p.astype(BF16), w_ref[...]) * sc_ref[...]
            y_ref[...] = y.astype(BF16)


def _pool_call(u3, col0, width, w, scale):
    b, n, _ = u3.shape
    gc = width // N_GROUPS
    assert max(POOL_WINDOWS) // 2 <= POOL_PAD
    return pl.pallas_call(
        functools.partial(_pool_body, n),
        grid=(b, N_GROUPS),
        in_specs=[pl.BlockSpec((None, n, gc), lambda bi, g: (bi, 0, col0 // gc + g)),
                  pl.BlockSpec((None, gc, gc), lambda bi, g: (g, 0, 0)),
                  pl.BlockSpec((1, gc), lambda bi, g: (0, g))],
        out_specs=pl.BlockSpec((None, n, gc), lambda bi, g: (bi, 0, g)),
        out_shape=jax.ShapeDtypeStruct((b, n, width), BF16),
        compiler_params=_params("parallel", "parallel"),
        name="pool_mix",
    )(u3, w, scale)


def _dft_tables(rows, cols, period):
    k = jnp.arange(rows, dtype=jnp.int32)
    s = jnp.arange(cols, dtype=jnp.int32)
    m = (k[:, None] * s[None, :]) % period
    ang = m.astype(F32) * (2.0 * math.pi / period)
    return jnp.cos(ang).astype(BF16), jnp.sin(ang).astype(BF16)


def _fnet_tables(n):
    top_c, top_s = _dft_tables(n // 2, n, n)
    sign = jnp.where(jnp.arange(n) % 2 == 0, 1.0, -1.0).astype(BF16)[None, :]
    return jnp.concatenate([top_c, top_c * sign], axis=0), jnp.concatenate([top_s, top_s * sign], axis=0)


def _grid_pos_embed(n_tokens, d):
    rows = n_tokens // GRID_W
    rr, cc = jnp.meshgrid(jnp.arange(rows, dtype=F32), jnp.arange(GRID_W, dtype=F32), indexing='ij')
    rr = rr.reshape(-1)[:, None]
    cc = cc.reshape(-1)[:, None]
    quarter = d // 4
    omega = 1.0 / (POS_BASE ** (jnp.arange(quarter, dtype=F32) / quarter))
    ar, ac = rr * omega, cc * omega
    return jnp.concatenate([jnp.sin(ar), jnp.cos(ar), jnp.sin(ac), jnp.cos(ac)], axis=-1)


def _hyena_pos_features(n):
    t_idx = jnp.arange(n, dtype=F32)
    t = jnp.linspace(0.0, 1.0, n, dtype=F32)
    bands = jnp.linspace(1e-4, HY_BANDS - 1, HY_BANDS, dtype=F32)
    ang = (2.0 * math.pi / n) * t_idx[:, None] * bands[None, :]
    z = jnp.concatenate([t[:, None], jnp.cos(ang), jnp.sin(ang)], axis=-1)
    return jnp.pad(z, ((0, 0), (0, LANE - z.shape[1])))


def _pad2(a, rows, cols):
    return jnp.pad(a, ((0, rows - a.shape[0]), (0, cols - a.shape[1])))


def kernel(x_prompt, x_sample, state_ret_fwd, state_ret_bwd, c, c_ctx, norm_g, mod_w, mod_b, ffn_a_in, ffn_a_out, ffn_b_in, ffn_b_out, ev_in_w, ev_out_w, fnet_w, hy_conv_w, hy_conv_b, hy_w1, hy_b1, hy_w2, hy_b2, hy_w_out, hy_freq, hy_decay, hy_bias, od_in_w, od_out_w, ret_log_decay_fwd, ret_log_decay_bwd, ret_gn, pool_w, pool_scale, final_norm):
    depth, d = norm_g.shape[0], norm_g.shape[2]
    half = d // 2
    dec_b = x_sample.shape[0]
    assert dec_b + 1 <= MOD_ROWS

    cond = jnp.concatenate([c, c_ctx[None, :]], axis=0)
    cond = jnp.pad(cond, ((0, MOD_ROWS - cond.shape[0]), (0, 0)))
    mod = _mod_call(cond, mod_w, mod_b).reshape(depth, MOD_ROWS, N_MOD, 1, d)

    bf = lambda a: a.astype(BF16)
    ffn_a_in, ffn_a_out, ffn_b_in, ffn_b_out = bf(ffn_a_in), bf(ffn_a_out), bf(ffn_b_in), bf(ffn_b_out)
    ev_in_w, ev_out_w, od_in_w, od_out_w = bf(ev_in_w), bf(ev_out_w), bf(od_in_w), bf(od_out_w)
    fnet_w, pool_w = bf(fnet_w), bf(pool_w)

    hid = hy_w1.shape[2]
    n_even = hy_w1.shape[0]
    hy_w1p = jnp.stack([_pad2(hy_w1[e], LANE, LANE) for e in range(n_even)])
    hy_w2p = jnp.stack([_pad2(hy_w2[e], LANE, LANE) for e in range(n_even)])
    hy_woutp = jnp.pad(hy_w_out, ((0, 0), (0, LANE - hid), (0, 0)))
    padv = lambda a: jnp.pad(a, ((0, 0), (0, LANE - hid)))[:, None, :]
    hy_b1p, hy_b2p, hy_freqp = padv(hy_b1), padv(hy_b2), padv(hy_freq)

    gc = half // N_GROUPS
    cs_chan = jnp.concatenate(_dft_tables(gc, gc, gc), axis=1)

    def trunk(x3, mod_rows, s0_f, s0_b, pos=None):
        b, n, _ = x3.shape
        t = b * n
        rows_per_mod = n if mod_rows.stop - mod_rows.start > 1 else t
        x = x3.reshape(t, d)
        cq, sq = _dft_tables(n, n, 2 * n)
        cn, sn = _fnet_tables(n)
        z_pos = _hyena_pos_features(n)
        s_f_out, s_b_out = [], []
        for l in range(depth):
            m = [mod[l, mod_rows, i] for i in range(N_MOD)]
            x = _ffn_call(x, norm_g[l, 0][None], m[0], m[1], m[2], ffn_a_in, ffn_a_out, l, rows_per_mod,
                          pos=pos if l == 0 else None)
            if l % 2 == 0:
                e = l // 2
                u3 = _inproj_call(x, norm_g[l, 1][None], m[3], m[4], ev_in_w, e, rows_per_mod).reshape(b, n, -1)
                zc = _fnet_chan_call(u3, cs_chan, half)
                ya = _fnet_seq_call(cn, sn, zc, fnet_w[e])
                x0, vv, vn = _hy_pre_call(u3, half, half, hy_conv_w[e], hy_conv_b[e][None])
                hs, hd, tn = _hy_filter_call(z_pos, hy_w1p[e], hy_b1p[e], hy_w2p[e], hy_b2p[e], hy_freqp[e],
                                             hy_woutp[e], hy_decay[e][None])
                tre, tim = _hy_spec_call(cq, sq, hs, hd)
                p = _hy_fwd_call(cq, sq, vv, tre, tim)
                yb = _hy_inv_call(cq, sq, p, x0, vv, vn, tn, hy_bias[e][None])
                w_out, w_idx = ev_out_w, e
            else:
                o = l // 2
                u3 = _inproj_call(x, norm_g[l, 1][None], m[3], m[4], od_in_w, o, rows_per_mod).reshape(b, n, -1)
                ya, s_f, s_b = _ret_call(u3, half, ret_log_decay_fwd[o], ret_log_decay_bwd[o], ret_gn[o][None],
                                         s0_f[:, o], s0_b[:, o])
                yb = _pool_call(u3, 4 * half, half, pool_w[o], pool_scale[o][None])
                s_f_out.append(s_f)
                s_b_out.append(s_b)
                w_out, w_idx = od_out_w, o
            x = _outproj_call(x, ya.reshape(t, half), yb.reshape(t, half), w_out, w_idx, m[5], rows_per_mod)
            x = _ffn_call(x, norm_g[l, 2][None], m[6], m[7], m[8], ffn_b_in, ffn_b_out, l, rows_per_mod,
                          final_g=final_norm[None] if l == depth - 1 else None)
        return x.reshape(b, n, d), jnp.stack(s_f_out, axis=1), jnp.stack(s_b_out, axis=1)

    zero_state = jnp.zeros((x_prompt.shape[0],) + state_ret_fwd.shape[1:], F32)
    y_prompt, st_f, st_b = trunk(x_prompt, slice(dec_b, dec_b + 1), zero_state, zero_state)

    y_sample, _, _ = trunk(x_sample, slice(0, dec_b), state_ret_fwd, state_ret_bwd,
                           pos=_grid_pos_embed(x_sample.shape[1], d))
    return (y_prompt, y_sample, st_f, st_b)
```

```python
import functools
import math

import jax
import jax.numpy as jnp
from jax import lax
from jax.experimental import pallas as pl
from jax.experimental.pallas import tpu as pltpu

F32 = jnp.float32
BF16 = jnp.bfloat16

EPS = 1e-6
N_MOD = 9
GRID_W = 64
POS_BASE = 10000.0
N_GROUPS = 4
HY_BANDS = 16
RET_CHUNK = 128
POOL_WINDOWS = (2, 4, 8, 16)
POOL_PAD = 8
LANE = 128
MOD_ROWS = 16
VMEM_LIMIT = 56 * 1024 * 1024
NORM_ROWS = 32
NORM_UNROLL = 4
FFN_TM = 1024
FFN_TF = 512


def _params(*sem):
    return pltpu.CompilerParams(dimension_semantics=sem, vmem_limit_bytes=VMEM_LIMIT)


def _tile(n, pref):
    if n <= pref:
        return n
    t = pref
    while n % t:
        t -= LANE
    assert t > 0, (n, pref)
    return t


def _silu(x):
    return x * (1.0 / (1.0 + jnp.exp(-x)))


def _norm_mod_into(h_ref, x_ref, g_ref, shift_ref, scale_ref):
    gs = g_ref[...] * (1.0 + scale_ref[...])
    shift = shift_ref[...]

    def body(r, carry):
        rows = pl.ds(pl.multiple_of(r * NORM_ROWS, NORM_ROWS), NORM_ROWS)
        x = x_ref[rows, :]
        ms = jnp.mean(x * x, axis=-1, keepdims=True)
        h_ref[rows, :] = (x * lax.rsqrt(ms + EPS) * gs + shift).astype(h_ref.dtype)
        return carry

    lax.fori_loop(0, x_ref.shape[0] // NORM_ROWS, body, 0, unroll=NORM_UNROLL)


def _dot(a, b):
    return jnp.dot(a, b, preferred_element_type=F32)


def _mod_body(c_ref, w_ref, b_ref, o_ref):
    s = _silu(c_ref[...]).astype(BF16)
    o_ref[...] = _dot(s, w_ref[...].astype(BF16)) + b_ref[...]


def _mod_call(cond, mod_w, mod_b):
    depth, d, n_out = mod_w.shape
    tn = _tile(n_out, 1024)
    return pl.pallas_call(
        _mod_body,
        grid=(depth, n_out // tn),
        in_specs=[
            pl.BlockSpec((MOD_ROWS, d), lambda l, j: (0, 0)),
            pl.BlockSpec((None, d, tn), lambda l, j: (l, 0, j)),
            pl.BlockSpec((None, 1, tn), lambda l, j: (l, 0, j)),
        ],
        out_specs=pl.BlockSpec((None, MOD_ROWS, tn), lambda l, j: (l, 0, j)),
        out_shape=jax.ShapeDtypeStruct((depth, MOD_ROWS, n_out), F32),
        compiler_params=_params("parallel", "parallel"),
        name="mod_proj",
    )(cond, mod_w, mod_b.reshape(depth, 1, n_out))


def _ffn_body(final, with_pos, x_ref, g_ref, sh_ref, sc_ref, gt_ref, wg_ref, wu_ref, wo_ref, *rest):
    rest = list(rest)
    pos_ref = rest.pop(0) if with_pos else None
    fg_ref = rest.pop(0) if final else None
    o_ref, h_ref = rest
    j = pl.program_id(1)

    @pl.when(j == 0)
    def _():
        if with_pos:
            o_ref[...] = x_ref[...] + pos_ref[...]
            _norm_mod_into(h_ref, o_ref, g_ref, sh_ref, sc_ref)
        else:
            _norm_mod_into(h_ref, x_ref, g_ref, sh_ref, sc_ref)
            o_ref[...] = x_ref[...]

    h = h_ref[...]
    gate = _dot(h, wg_ref[...])
    up = _dot(h, wu_ref[...])
    a = (_silu(gate) * up).astype(BF16)
    o_ref[...] += (0.5 * gt_ref[...]) * _dot(a, wo_ref[...])

    if final:
        @pl.when(j == pl.num_programs(1) - 1)
        def _():
            _rmsnorm_rows(o_ref, fg_ref)


def _rmsnorm_rows(o_ref, g_ref):
    g = g_ref[...]

    def body(r, carry):
        rows = pl.ds(pl.multiple_of(r * NORM_ROWS, NORM_ROWS), NORM_ROWS)
        y = o_ref[rows, :]
        ms = jnp.mean(y * y, axis=-1, keepdims=True)
        o_ref[rows, :] = y * lax.rsqrt(ms + EPS) * g
        return carry

    lax.fori_loop(0, o_ref.shape[0] // NORM_ROWS, body, 0, unroll=NORM_UNROLL)


def _ffn_call(x, g, shift, scale, gate, w_in, w_out, layer, rows_per_mod, final_g=None, pos=None):
    t, d = x.shape
    d_ff = w_out.shape[1]
    tm = _tile(rows_per_mod, FFN_TM // 2 if pos is not None else FFN_TM)
    tf = _tile(d_ff, FFN_TF)
    nf = d_ff // tf
    final = final_g is not None
    with_pos = pos is not None
    tiles_per_mod = rows_per_mod // tm
    mod_spec = pl.BlockSpec((None, 1, d), lambda i, j: (i // tiles_per_mod, 0, 0))
    vec_spec = pl.BlockSpec((1, d), lambda i, j: (0, 0))
    in_specs = [
        pl.BlockSpec((tm, d), lambda i, j: (i, 0)),
        vec_spec, mod_spec, mod_spec, mod_spec,
        pl.BlockSpec((None, d, tf), lambda i, j: (layer, 0, j)),
        pl.BlockSpec((None, d, tf), lambda i, j: (layer, 0, j + nf)),
        pl.BlockSpec((None, tf, d), lambda i, j: (layer, j, 0)),
    ]
    args = [x, g, shift, scale, gate, w_in, w_in, w_out]
    if with_pos:
        in_specs.append(pl.BlockSpec((tm, d), lambda i, j: (i % tiles_per_mod, 0)))
        args.append(pos)
    if final:
        in_specs.append(vec_spec)
        args.append(final_g)
    return pl.pallas_call(
        functools.partial(_ffn_body, final, with_pos),
        grid=(t // tm, nf),
        in_specs=in_specs,
        out_specs=pl.BlockSpec((tm, d), lambda i, j: (i, 0)),
        out_shape=jax.ShapeDtypeStruct((t, d), F32),
        scratch_shapes=[pltpu.VMEM((tm, d), BF16)],
        compiler_params=_params("parallel", "arbitrary"),
        name="ffn",
    )(*args)


def _inproj_body(x_ref, g_ref, sh_ref, sc_ref, w_ref, o_ref, h_ref):
    @pl.when(pl.program_id(1) == 0)
    def _():
        _norm_mod_into(h_ref, x_ref, g_ref, sh_ref, sc_ref)

    o_ref[...] = _dot(h_ref[...], w_ref[...]).astype(BF16)


def _inproj_call(x, g, shift, scale, w, layer, rows_per_mod):
    t, d = x.shape
    n_out = w.shape[2]
    tm = _tile(rows_per_mod, 1024)
    tn = _tile(n_out, 1024)
    mod_spec = pl.BlockSpec((None, 1, d), lambda i, j: ((i * tm) // rows_per_mod, 0, 0))
    return pl.pallas_call(
        _inproj_body,
        grid=(t // tm, n_out // tn),
        in_specs=[
            pl.BlockSpec((tm, d), lambda i, j: (i, 0)),
            pl.BlockSpec((1, d), lambda i, j: (0, 0)),
            mod_spec, mod_spec,
            pl.BlockSpec((None, d, tn), lambda i, j: (layer, 0, j)),
        ],
        out_specs=pl.BlockSpec((tm, tn), lambda i, j: (i, j)),
        out_shape=jax.ShapeDtypeStruct((t, n_out), BF16),
        scratch_shapes=[pltpu.VMEM((tm, d), BF16)],
        compiler_params=_params("parallel", "arbitrary"),
        name="mix_in_proj",
    )(x, g, shift, scale, w)


def _outproj_body(x_ref, ya_ref, yb_ref, wa_ref, wb_ref, gt_ref, o_ref):
    y = _dot(ya_ref[...], wa_ref[...]) + _dot(yb_ref[...], wb_ref[...])
    o_ref[...] = x_ref[...] + gt_ref[...] * y


def _outproj_call(x, ya, yb, w, layer, gate, rows_per_mod):
    t, d = x.shape
    half = ya.shape[1]
    tm = _tile(rows_per_mod, 512)
    return pl.pallas_call(
        _outproj_body,
        grid=(t // tm,),
        in_specs=[
            pl.BlockSpec((tm, d), lambda i: (i, 0)),
            pl.BlockSpec((tm, half), lambda i: (i, 0)),
            pl.BlockSpec((tm, half), lambda i: (i, 0)),
            pl.BlockSpec((None, half, d), lambda i: (layer, 0, 0)),
            pl.BlockSpec((None, half, d), lambda i: (layer, 1, 0)),
            pl.BlockSpec((None, 1, d), lambda i: ((i * tm) // rows_per_mod, 0, 0)),
        ],
        out_specs=pl.BlockSpec((tm, d), lambda i: (i, 0)),
        out_shape=jax.ShapeDtypeStruct((t, d), F32),
        compiler_params=_params("parallel"),
        name="mix_out_proj",
    )(x, ya, yb, w, w, gate)


def _fnet_chan_body(gc, x_ref, cs_ref, z_ref):
    half = N_GROUPS * gc
    for g in range(N_GROUPS):
        xg = x_ref[:, g * gc:(g + 1) * gc].astype(BF16)
        z = _dot(xg, cs_ref[...])
        z_ref[:, g * gc:(g + 1) * gc] = z[:, :gc].astype(BF16)
        z_ref[:, half + g * gc:half + (g + 1) * gc] = z[:, gc:].astype(BF16)


def _fnet_chan_call(u3, cs, width):
    b, n, _ = u3.shape
    gc = width // N_GROUPS
    tr = _tile(n, 512)
    return pl.pallas_call(
        functools.partial(_fnet_chan_body, gc),
        grid=(b, n // tr),
        in_specs=[
            pl.BlockSpec((None, tr, width), lambda bi, i: (bi, i, 0)),
            pl.BlockSpec((gc, 2 * gc), lambda bi, i: (0, 0)),
        ],
        out_specs=pl.BlockSpec((None, tr, 2 * width), lambda bi, i: (bi, i, 0)),
        out_shape=jax.ShapeDtypeStruct((b, n, 2 * width), BF16),
        compiler_params=_params("parallel", "parallel"),
        name="fnet_chan_dft",
    )(u3, cs)


def _fnet_seq_body(gc, scale, c_ref, s_ref, z_ref, w_ref, o_ref, acc_ref):
    k = pl.program_id(2)
    half = N_GROUPS * gc

    @pl.when(k == 0)
    def _():
        acc_ref[...] = jnp.zeros_like(acc_ref)

    acc_ref[...] += _dot(c_ref[...], z_ref[:, :half]) - _dot(s_ref[...], z_ref[:, half:])

    @pl.when(k == pl.num_programs(2) - 1)
    def _():
        for g in range(N_GROUPS):
            r = (acc_ref[:, g * gc:(g + 1) * gc] * scale).astype(BF16)
            o_ref[:, g * gc:(g + 1) * gc] = _dot(r, w_ref[g]).astype(BF16)


def _fnet_seq_call(cn, sn, z, w):
    b, n, two_w = z.shape
    width = two_w // 2
    gc = width // N_GROUPS
    tm = _tile(n, 1024)
    tk = _tile(n, 512)
    scale = 1.0 / math.sqrt(n * gc)
    return pl.pallas_call(
        functools.partial(_fnet_seq_body, gc, scale),
        grid=(b, n // tm, n // tk),
        in_specs=[
            pl.BlockSpec((tm, tk), lambda bi, i, k: (i, k)),
            pl.BlockSpec((tm, tk), lambda bi, i, k: (i, k)),
            pl.BlockSpec((None, tk, two_w), lambda bi, i, k: (bi, k, 0)),
            pl.BlockSpec((N_GROUPS, gc, gc), lambda bi, i, k: (0, 0, 0)),
        ],
        out_specs=pl.BlockSpec((None, tm, width), lambda bi, i, k: (bi, i, 0)),
        out_shape=jax.ShapeDtypeStruct((b, n, width), BF16),
        scratch_shapes=[pltpu.VMEM((tm, width), F32)],
        compiler_params=_params("parallel", "parallel", "arbitrary"),
        name="fnet_seq_dft",
    )(cn, sn, z, w)


def _hy_pre_body(n, u0_ref, u1_ref, u2_ref, w0_ref, w1_ref, w2_ref, b0_ref, b1_ref, b2_ref,
                 x0_ref, vv_ref, vn_ref):
    row = lax.broadcasted_iota(jnp.int32, u0_ref.shape, 0)

    def conv(u_ref, w_ref, b_ref):
        u = u_ref[...].astype(F32)
        prev = jnp.where(row == 0, 0.0, pltpu.roll(u, 1, 0))
        nxt = jnp.where(row == n - 1, 0.0, pltpu.roll(u, n - 1, 0))
        return prev * w_ref[0:1, :] + u * w_ref[1:2, :] + nxt * w_ref[2:3, :] + b_ref[...]

    x0_ref[...] = conv(u0_ref, w0_ref, b0_ref).astype(BF16)
    vv = conv(u2_ref, w2_ref, b2_ref) * conv(u1_ref, w1_ref, b1_ref)
    vv_ref[...] = vv.astype(BF16)
    vn_ref[...] = jnp.sum(jnp.where(row % 2 == 0, vv, -vv), axis=0, keepdims=True)


def _hy_pre_call(u3, col0, width, conv_w, conv_b):
    b, n, _ = u3.shape
    tc = _tile(width, LANE)
    nct = width // tc

    def u_spec(s):
        return pl.BlockSpec((None, n, tc), lambda bi, c: (bi, 0, (col0 + s * width) // tc + c))

    def w_spec(s):
        return pl.BlockSpec((3, tc), lambda bi, c: (0, s * nct + c))

    def b_spec(s):
        return pl.BlockSpec((1, tc), lambda bi, c: (0, s * nct + c))

    out3 = pl.BlockSpec((None, n, tc), lambda bi, c: (bi, 0, c))
    return pl.pallas_call(
        functools.partial(_hy_pre_body, n),
        grid=(b, nct),
        in_specs=[u_spec(0), u_spec(1), u_spec(2), w_spec(0), w_spec(1), w_spec(2),
                  b_spec(0), b_spec(1), b_spec(2)],
        out_specs=[out3, out3, pl.BlockSpec((None, 1, tc), lambda bi, c: (bi, 0, c))],
        out_shape=[jax.ShapeDtypeStruct((b, n, width), BF16),
                   jax.ShapeDtypeStruct((b, n, width), BF16),
                   jax.ShapeDtypeStruct((b, 1, width), F32)],
        compiler_params=_params("parallel", "parallel"),
        name="hyena_short_conv",
    )(u3, u3, u3, conv_w, conv_w, conv_w, conv_b, conv_b, conv_b)


def _hy_filter_body(z_ref, w1_ref, b1_ref, w2_ref, b2_ref, fr_ref, wf_ref, wb_ref, dec_ref,
                    hs_ref, hd_ref, tn_ref):
    hp = lax.Precision.HIGHEST
    z = z_ref[...]
    fr = fr_ref[...]
    h = jnp.sin(fr * (jnp.dot(z, w1_ref[...], precision=hp, preferred_element_type=F32) + b1_ref[...]))
    h = jnp.sin(fr * (jnp.dot(h, w2_ref[...], precision=hp, preferred_element_type=F32) + b2_ref[...]))
    window = jnp.exp(-z[:, 0:1] * jnp.abs(dec_ref[...]))
    hf = jnp.dot(h, wf_ref[...], precision=hp, preferred_element_type=F32) * window
    hb = jnp.dot(h, wb_ref[...], precision=hp, preferred_element_type=F32) * window
    row = lax.broadcasted_iota(jnp.int32, hf.shape, 0)
    hb = jnp.where(row == 0, 0.0, hb)
    l1 = jnp.sum(jnp.abs(hf), axis=0, keepdims=True) + jnp.sum(jnp.abs(hb), axis=0, keepdims=True)
    hs = (hf + hb) / l1
    hd = (hb - hf) / l1
    hs_ref[...] = hs.astype(BF16)
    hd_ref[...] = hd.astype(BF16)
    tn_ref[...] = jnp.sum(jnp.where(row % 2 == 0, hs, -hs), axis=0, keepdims=True)


def _hy_filter_call(z, w1, b1, w2, b2, freq, w_out, decay):
    n, kz = z.shape
    hid = w1.shape[1]
    width = decay.shape[1]
    tc = _tile(width, 256)
    nct = width // tc
    full = lambda a: pl.BlockSpec(a.shape, lambda c: (0, 0))
    col = pl.BlockSpec((n, tc), lambda c: (0, c))
    vec = pl.BlockSpec((1, tc), lambda c: (0, c))
    return pl.pallas_call(
        _hy_filter_body,
        grid=(nct,),
        in_specs=[full(z), full(w1), full(b1), full(w2), full(b2), full(freq),
                  pl.BlockSpec((hid, tc), lambda c: (0, c)),
                  pl.BlockSpec((hid, tc), lambda c: (0, c + nct)),
                  vec],
        out_specs=[col, col, vec],
        out_shape=[jax.ShapeDtypeStruct((n, width), BF16),
                   jax.ShapeDtypeStruct((n, width), BF16),
                   jax.ShapeDtypeStruct((1, width), F32)],
        compiler_params=_params("parallel"),
        name="hyena_filter",
    )(z, w1, b1, w2, b2, freq, w_out, w_out, decay)


def _hy_spec_body(c_ref, s_ref, hs_ref, hd_ref, tre_ref, tim_ref):
    @pl.when(pl.program_id(1) == 0)
    def _():
        tre_ref[...] = jnp.zeros_like(tre_ref)
        tim_ref[...] = jnp.zeros_like(tim_ref)

    tre_ref[...] += _dot(c_ref[...], hs_ref[...])
    tim_ref[...] += _dot(s_ref[...], hd_ref[...])


def _hy_spec_call(cq, sq, hs, hd):
    n, width = hs.shape
    tm = _tile(n, 512)
    tk = _tile(n, 512)
    mat = pl.BlockSpec((tm, tk), lambda i, k: (i, k))
    rhs = pl.BlockSpec((tk, width), lambda i, k: (k, 0))
    out = pl.BlockSpec((tm, width), lambda i, k: (i, 0))
    return pl.pallas_call(
        _hy_spec_body,
        grid=(n // tm, n // tk),
        in_specs=[mat, mat, rhs, rhs],
        out_specs=[out, out],
        out_shape=[jax.ShapeDtypeStruct((n, width), F32)] * 2,
        compiler_params=_params("parallel", "arbitrary"),
        name="hyena_filter_spectrum",
    )(cq, sq, hs, hd)


def _hy_fwd_body(n, width, c_ref, s_ref, v_ref, tre_ref, tim_ref, p_ref, ar_ref, ai_ref):
    k = pl.program_id(2)

    @pl.when(k == 0)
    def _():
        ar_ref[...] = jnp.zeros_like(ar_ref)
        ai_ref[...] = jnp.zeros_like(ai_ref)

    v = v_ref[...]
    ar_ref[...] += _dot(c_ref[...], v)
    ai_ref[...] += _dot(s_ref[...], v)

    @pl.when(k == pl.num_programs(2) - 1)
    def _():
        tm = ar_ref.shape[0]
        freq = lax.broadcasted_iota(jnp.int32, (tm, 1), 0) + pl.program_id(1) * tm
        wk = jnp.where(freq == 0, 0.5 / n, 1.0 / n)
        ar, ai, tre, tim = ar_ref[...], ai_ref[...], tre_ref[...], tim_ref[...]
        p_ref[:, :width] = ((ar * tre + ai * tim) * wk).astype(BF16)
        p_ref[:, width:] = ((ar * tim - ai * tre) * wk).astype(BF16)


def _hy_fwd_call(cq, sq, vv, tre, tim):
    b, n, width = vv.shape
    tm = _tile(n, 1024)
    tk = _tile(n, 512)
    mat = pl.BlockSpec((tm, tk), lambda bi, i, k: (i, k))
    spec = pl.BlockSpec((tm, width), lambda bi, i, k: (i, 0))
    return pl.pallas_call(
        functools.partial(_hy_fwd_body, n, width),
        grid=(b, n // tm, n // tk),
        in_specs=[mat, mat, pl.BlockSpec((None, tk, width), lambda bi, i, k: (bi, k, 0)), spec, spec],
        out_specs=pl.BlockSpec((None, tm, 2 * width), lambda bi, i, k: (bi, i, 0)),
        out_shape=jax.ShapeDtypeStruct((b, n, 2 * width), BF16),
        scratch_shapes=[pltpu.VMEM((tm, width), F32), pltpu.VMEM((tm, width), F32)],
        compiler_params=_params("parallel", "parallel", "arbitrary"),
        name="hyena_fwd_dft",
    )(cq, sq, vv, tre, tim)


def _hy_inv_body(n, width, c_ref, s_ref, p_ref, x0_ref, vv_ref, vn_ref, tn_ref, bias_ref, o_ref, acc_ref):
    k = pl.program_id(2)

    @pl.when(k == 0)
    def _():
        acc_ref[...] = jnp.zeros_like(acc_ref)

    acc_ref[...] += _dot(c_ref[...], p_ref[:, :width]) - _dot(s_ref[...], p_ref[:, width:])

    @pl.when(k == pl.num_programs(2) - 1)
    def _():
        tm = acc_ref.shape[0]
        pos = lax.broadcasted_iota(jnp.int32, (tm, 1), 0) + pl.program_id(1) * tm
        sign = jnp.where(pos % 2 == 0, 1.0, -1.0)
        nyq = (vn_ref[...] * tn_ref[...]) * (0.5 / n)
        conv = acc_ref[...] + sign * nyq
        o_ref[...] = (x0_ref[...].astype(F32) * (conv + vv_ref[...].astype(F32) * bias_ref[...])).astype(BF16)


def _hy_inv_call(cq, sq, p, x0, vv, vn, tn, bias):
    b, n, width = vv.shape
    tm = _tile(n, 1024)
    tk = _tile(n, 512)
    mat = pl.BlockSpec((tm, tk), lambda bi, i, k: (i, k))
    tok = pl.BlockSpec((None, tm, width), lambda bi, i, k: (bi, i, 0))
    vec = pl.BlockSpec((1, width), lambda bi, i, k: (0, 0))
    return pl.pallas_call(
        functools.partial(_hy_inv_body, n, width),
        grid=(b, n // tm, n // tk),
        in_specs=[mat, mat, pl.BlockSpec((None, tk, 2 * width), lambda bi, i, k: (bi, k, 0)),
                  tok, tok, pl.BlockSpec((None, 1, width), lambda bi, i, k: (bi, 0, 0)), vec, vec],
        out_specs=tok,
        out_shape=jax.ShapeDtypeStruct((b, n, width), BF16),
        scratch_shapes=[pltpu.VMEM((tm, width), F32)],
        compiler_params=_params("parallel", "parallel", "arbitrary"),
        name="hyena_inv_dft",
    )(cq, sq, p, x0, vv, vn, tn, bias)


def _ret_body(dk, nc, lgf_ref, lgb_ref, q_ref, k_ref, v_ref, g_ref, gn_ref, s0f_ref, s0b_ref,
              y_ref, sfo_ref, sbo_ref, o_ref, sf_ref, sb_ref):
    h = pl.program_id(1)
    ch = RET_CHUNK
    sf_ref[...] = s0f_ref[...]
    sb_ref[...] = s0b_ref[...]

    ii = lax.broadcasted_iota(jnp.int32, (ch, ch), 0)
    jj = lax.broadcasted_iota(jnp.int32, (ch, ch), 1)
    diff = (ii - jj).astype(F32)
    idx = lax.broadcasted_iota(jnp.int32, (ch, 1), 0).astype(F32)
    k_scale = dk ** -0.5

    def decays(lg, lag, q_pow, k_pow):
        inner = jnp.where(lag >= 0, jnp.exp(lg * jnp.maximum(lag, 0.0)), 0.0)
        return inner, jnp.exp(lg * q_pow), jnp.exp(lg * k_pow), jnp.exp(lg * jnp.full((1, 1), float(ch), F32))

    dec_f = decays(lgf_ref[h], diff, idx + 1.0, (ch - 1.0) - idx)
    dec_b = decays(lgb_ref[h], -diff, float(ch) - idx, idx)

    def rows(c):
        return pl.ds(pl.multiple_of(c * ch, ch), ch)

    def chunk(c, dec, s_ref):
        inner_decay, q_decay, k_decay, chunk_decay = dec
        q = q_ref[rows(c), :]
        k = k_ref[rows(c), :].astype(F32) * k_scale
        v = v_ref[rows(c), :]
        scores = lax.dot_general(q, k.astype(BF16), (((1,), (1,)), ((), ())),
                                 preferred_element_type=F32) * inner_decay
        s = s_ref[...]
        out = _dot(scores.astype(BF16), v) + _dot(q, s.astype(BF16)) * q_decay
        kd = (k * k_decay).T.astype(BF16)
        s_ref[...] = s * chunk_decay + _dot(kd, v)
        return out

    def finish(c, o):
        mu = jnp.mean(o, axis=-1, keepdims=True)
        var = jnp.mean(jnp.square(o - mu), axis=-1, keepdims=True)
        on = (o - mu) * lax.rsqrt(var + EPS) * gn_ref[...]
        y_ref[rows(c), :] = (_silu(g_ref[rows(c), :].astype(F32)) * on).astype(BF16)

    def first_half(c, carry):
        cb = nc - 1 - c
        o_ref[rows(c), :] = chunk(c, dec_f, sf_ref)
        o_ref[rows(cb), :] = chunk(cb, dec_b, sb_ref)
        return carry

    def second_half(c, carry):
        cb = nc - 1 - c
        finish(c, o_ref[rows(c), :] + chunk(c, dec_f, sf_ref))
        finish(cb, o_ref[rows(cb), :] + chunk(cb, dec_b, sb_ref))
        return carry

    lax.fori_loop(0, nc // 2, first_half, 0)
    lax.fori_loop(nc // 2, nc, second_half, 0)
    sfo_ref[...] = sf_ref[...]
    sbo_ref[...] = sb_ref[...]


def _ret_call(u3, width, lgf, lgb, gn, s0f, s0b):
    b, n, _ = u3.shape
    dk = width // N_GROUPS
    nc = n // RET_CHUNK
    assert nc % 2 == 0

    def tok(part):
        return pl.BlockSpec((None, n, dk), lambda bi, h: (bi, 0, part * N_GROUPS + h))

    state = pl.BlockSpec((None, None, dk, dk), lambda bi, h: (bi, h, 0, 0))
    smem = pl.BlockSpec(memory_space=pltpu.SMEM)
    return pl.pallas_call(
        functools.partial(_ret_body, dk, nc),
        grid=(b, N_GROUPS),
        in_specs=[smem, smem, tok(0), tok(1), tok(2), tok(3),
                  pl.BlockSpec((1, dk), lambda bi, h: (0, h)), state, state],
        out_specs=[pl.BlockSpec((None, n, dk), lambda bi, h: (bi, 0, h)), state, state],
        out_shape=[jax.ShapeDtypeStruct((b, n, width), BF16),
                   jax.ShapeDtypeStruct((b, N_GROUPS, dk, dk), F32),
                   jax.ShapeDtypeStruct((b, N_GROUPS, dk, dk), F32)],
        scratch_shapes=[pltpu.VMEM((n, dk), F32), pltpu.VMEM((dk, dk), F32), pltpu.VMEM((dk, dk), F32)],
        compiler_params=_params("parallel", "parallel"),
        name="retention",
    )(lgf, lgb, u3, u3, u3, u3, gn, s0f, s0b)


def _pool_body(n, x_ref, w_ref, sc_ref, y_ref):
    g = pl.program_id(1)
    gc = x_ref.shape[1]
    n_pad = n + 2 * POOL_PAD
    pos = lax.broadcasted_iota(jnp.int32, (n, 1), 0)

    for gi, win in enumerate(POOL_WINDOWS):
        @pl.when(g == gi)
        def _(win=win):
            x = x_ref[...].astype(F32)
            zeros = jnp.zeros((POOL_PAD, gc), F32)
            s = jnp.concatenate([zeros, x, zeros], axis=0)
            span = 1
            while span < win:
                s = s + pltpu.roll(s, span, 0)
                span *= 2
            total = pltpu.roll(s, n_pad - (POOL_PAD + win // 2 - 1), 0)[:n]
            lo = jnp.clip(pos - win // 2, 0, n - 1)
            hi = jnp.clip(pos - win // 2 + win - 1, 0, n - 1)
            count = (hi - lo + 1).astype(F32)
            p = total / count - x
            y = _dot(p.astype(BF16), w_ref[...]) * sc_ref[...]
            y_ref[...] = y.astype(BF16)


def _pool_call(u3, col0, width, w, scale):
    b, n, _ = u3.shape
    gc = width // N_GROUPS
    assert max(POOL_WINDOWS) // 2 <= POOL_PAD
    return pl.pallas_call(
        functools.partial(_pool_body, n),
        grid=(b, N_GROUPS),
        in_specs=[pl.BlockSpec((None, n, gc), lambda bi, g: (bi, 0, col0 // gc + g)),
                  pl.BlockSpec((None, gc, gc), lambda bi, g: (g, 0, 0)),
                  pl.BlockSpec((1, gc), lambda bi, g: (0, g))],
        out_specs=pl.BlockSpec((None, n, gc), lambda bi, g: (bi, 0, g)),
        out_shape=jax.ShapeDtypeStruct((b, n, width), BF16),
        compiler_params=_params("parallel", "parallel"),
        name="pool_mix",
    )(u3, w, scale)


def _dft_tables(rows, cols, period):
    k = jnp.arange(rows, dtype=jnp.int32)
    s = jnp.arange(cols, dtype=jnp.int32)
    m = (k[:, None] * s[None, :]) % period
    ang = m.astype(F32) * (2.0 * math.pi / period)
    return jnp.cos(ang).astype(BF16), jnp.sin(ang).astype(BF16)


def _fnet_tables(n):
    top_c, top_s = _dft_tables(n // 2, n, n)
    sign = jnp.where(jnp.arange(n) % 2 == 0, 1.0, -1.0).astype(BF16)[None, :]
    return jnp.concatenate([top_c, top_c * sign], axis=0), jnp.concatenate([top_s, top_s * sign], axis=0)


def _grid_pos_embed(n_tokens, d):
    rows = n_tokens // GRID_W
    rr, cc = jnp.meshgrid(jnp.arange(rows, dtype=F32), jnp.arange(GRID_W, dtype=F32), indexing='ij')
    rr = rr.reshape(-1)[:, None]
    cc = cc.reshape(-1)[:, None]
    quarter = d // 4
    omega = 1.0 / (POS_BASE ** (jnp.arange(quarter, dtype=F32) / quarter))
    ar, ac = rr * omega, cc * omega
    return jnp.concatenate([jnp.sin(ar), jnp.cos(ar), jnp.sin(ac), jnp.cos(ac)], axis=-1)


def _hyena_pos_features(n):
    t_idx = jnp.arange(n, dtype=F32)
    t = jnp.linspace(0.0, 1.0, n, dtype=F32)
    bands = jnp.linspace(1e-4, HY_BANDS - 1, HY_BANDS, dtype=F32)
    ang = (2.0 * math.pi / n) * t_idx[:, None] * bands[None, :]
    z = jnp.concatenate([t[:, None], jnp.cos(ang), jnp.sin(ang)], axis=-1)
    return jnp.pad(z, ((0, 0), (0, LANE - z.shape[1])))


def _pad2(a, rows, cols):
    return jnp.pad(a, ((0, rows - a.shape[0]), (0, cols - a.shape[1])))


def kernel(x_prompt, x_sample, state_ret_fwd, state_ret_bwd, c, c_ctx, norm_g, mod_w, mod_b, ffn_a_in, ffn_a_out, ffn_b_in, ffn_b_out, ev_in_w, ev_out_w, fnet_w, hy_conv_w, hy_conv_b, hy_w1, hy_b1, hy_w2, hy_b2, hy_w_out, hy_freq, hy_decay, hy_bias, od_in_w, od_out_w, ret_log_decay_fwd, ret_log_decay_bwd, ret_gn, pool_w, pool_scale, final_norm):
    depth, d = norm_g.shape[0], norm_g.shape[2]
    half = d // 2
    dec_b = x_sample.shape[0]
    assert dec_b + 1 <= MOD_ROWS

    cond = jnp.concatenate([c, c_ctx[None, :]], axis=0)
    cond = jnp.pad(cond, ((0, MOD_ROWS - cond.shape[0]), (0, 0)))
    mod = _mod_call(cond, mod_w, mod_b).reshape(depth, MOD_ROWS, N_MOD, 1, d)

    bf = lambda a: a.astype(BF16)
    ffn_a_in, ffn_a_out, ffn_b_in, ffn_b_out = bf(ffn_a_in), bf(ffn_a_out), bf(ffn_b_in), bf(ffn_b_out)
    ev_in_w, ev_out_w, od_in_w, od_out_w = bf(ev_in_w), bf(ev_out_w), bf(od_in_w), bf(od_out_w)
    fnet_w, pool_w = bf(fnet_w), bf(pool_w)

    hid = hy_w1.shape[2]
    n_even = hy_w1.shape[0]
    hy_w1p = jnp.stack([_pad2(hy_w1[e], LANE, LANE) for e in range(n_even)])
    hy_w2p = jnp.stack([_pad2(hy_w2[e], LANE, LANE) for e in range(n_even)])
    hy_woutp = jnp.pad(hy_w_out, ((0, 0), (0, LANE - hid), (0, 0)))
    padv = lambda a: jnp.pad(a, ((0, 0), (0, LANE - hid)))[:, None, :]
    hy_b1p, hy_b2p, hy_freqp = padv(hy_b1), padv(hy_b2), padv(hy_freq)

    gc = half // N_GROUPS
    cs_chan = jnp.concatenate(_dft_tables(gc, gc, gc), axis=1)

    def trunk(x3, mod_rows, s0_f, s0_b, pos=None):
        b, n, _ = x3.shape
        t = b * n
        rows_per_mod = n if mod_rows.stop - mod_rows.start > 1 else t
        x = x3.reshape(t, d)
        cq, sq = _dft_tables(n, n, 2 * n)
        cn, sn = _fnet_tables(n)
        z_pos = _hyena_pos_features(n)
        s_f_out, s_b_out = [], []
        for l in range(depth):
            m = [mod[l, mod_rows, i] for i in range(N_MOD)]
            x = _ffn_call(x, norm_g[l, 0][None], m[0], m[1], m[2], ffn_a_in, ffn_a_out, l, rows_per_mod,
                          pos=pos if l == 0 else None)
            if l % 2 == 0:
                e = l // 2
                u3 = _inproj_call(x, norm_g[l, 1][None], m[3], m[4], ev_in_w, e, rows_per_mod).reshape(b, n, -1)
                zc = _fnet_chan_call(u3, cs_chan, half)
                ya = _fnet_seq_call(cn, sn, zc, fnet_w[e])
                x0, vv, vn = _hy_pre_call(u3, half, half, hy_conv_w[e], hy_conv_b[e][None])
                hs, hd, tn = _hy_filter_call(z_pos, hy_w1p[e], hy_b1p[e], hy_w2p[e], hy_b2p[e], hy_freqp[e],
                                             hy_woutp[e], hy_decay[e][None])
                tre, tim = _hy_spec_call(cq, sq, hs, hd)
                p = _hy_fwd_call(cq, sq, vv, tre, tim)
                yb = _hy_inv_call(cq, sq, p, x0, vv, vn, tn, hy_bias[e][None])
                w_out, w_idx = ev_out_w, e
            else:
                o = l // 2
                u3 = _inproj_call(x, norm_g[l, 1][None], m[3], m[4], od_in_w, o, rows_per_mod).reshape(b, n, -1)
                ya, s_f, s_b = _ret_call(u3, half, ret_log_decay_fwd[o], ret_log_decay_bwd[o], ret_gn[o][None],
                                         s0_f[:, o], s0_b[:, o])
                yb = _pool_call(u3, 4 * half, half, pool_w[o], pool_scale[o][None])
                s_f_out.append(s_f)
                s_b_out.append(s_b)
                w_out, w_idx = od_out_w, o
            x = _outproj_call(x, ya.reshape(t, half), yb.reshape(t, half), w_out, w_idx, m[5], rows_per_mod)
            x = _ffn_call(x, norm_g[l, 2][None], m[6], m[7], m[8], ffn_b_in, ffn_b_out, l, rows_per_mod,
                          final_g=final_norm[None] if l == depth - 1 else None)
        return x.reshape(b, n, d), jnp.stack(s_f_out, axis=1), jnp.stack(s_b_out, axis=1)

    zero_state = jnp.zeros((x_prompt.shape[0],) + state_ret_fwd.shape[1:], F32)
    y_prompt, st_f, st_b = trunk(x_prompt, slice(dec_b, dec_b + 1), zero_state, zero_state)

    y_sample, _, _ = trunk(x_sample, slice(0, dec_b), state_ret_fwd, state_ret_bwd,
                           pos=_grid_pos_embed(x_sample.shape[1], d))
    return (y_prompt, y_sample, st_f, st_b)
```

```python
import functools
import math

import jax
import jax.numpy as jnp
from jax import lax
from jax.experimental import pallas as pl
from jax.experimental.pallas import tpu as pltpu

F32 = jnp.float32
BF16 = jnp.bfloat16

EPS = 1e-6
N_MOD = 9
GRID_W = 64
POS_BASE = 10000.0
N_GROUPS = 4
HY_BANDS = 16
RET_CHUNK = 256
POOL_WINDOWS = (2, 4, 8, 16)
POOL_PAD = 8
LANE = 128
MOD_ROWS = 16
VMEM_LIMIT = 56 * 1024 * 1024
NORM_ROWS = 32
NORM_UNROLL = 4
FFN_TM = 1024
FFN_TF = 512


def _params(*sem):
    return pltpu.CompilerParams(dimension_semantics=sem, vmem_limit_bytes=VMEM_LIMIT)


def _tile(n, pref):
    if n <= pref:
        return n
    t = pref
    while n % t:
        t -= LANE
    assert t > 0, (n, pref)
    return t


def _silu(x):
    return x * (1.0 / (1.0 + jnp.exp(-x)))


def _norm_mod_into(h_ref, x_ref, g_ref, shift_ref, scale_ref):
    gs = g_ref[...] * (1.0 + scale_ref[...])
    shift = shift_ref[...]

    def body(r, carry):
        rows = pl.ds(pl.multiple_of(r * NORM_ROWS, NORM_ROWS), NORM_ROWS)
        x = x_ref[rows, :]
        ms = jnp.mean(x * x, axis=-1, keepdims=True)
        h_ref[rows, :] = (x * lax.rsqrt(ms + EPS) * gs + shift).astype(h_ref.dtype)
        return carry

    lax.fori_loop(0, x_ref.shape[0] // NORM_ROWS, body, 0, unroll=NORM_UNROLL)


def _dot(a, b):
    return jnp.dot(a, b, preferred_element_type=F32)


def _mod_body(c_ref, w_ref, b_ref, o_ref):
    s = _silu(c_ref[...]).astype(BF16)
    o_ref[...] = _dot(s, w_ref[...].astype(BF16)) + b_ref[...]


def _mod_call(cond, mod_w, mod_b):
    depth, d, n_out = mod_w.shape
    tn = _tile(n_out, 1024)
    return pl.pallas_call(
        _mod_body,
        grid=(depth, n_out // tn),
        in_specs=[
            pl.BlockSpec((MOD_ROWS, d), lambda l, j: (0, 0)),
            pl.BlockSpec((None, d, tn), lambda l, j: (l, 0, j)),
            pl.BlockSpec((None, 1, tn), lambda l, j: (l, 0, j)),
        ],
        out_specs=pl.BlockSpec((None, MOD_ROWS, tn), lambda l, j: (l, 0, j)),
        out_shape=jax.ShapeDtypeStruct((depth, MOD_ROWS, n_out), F32),
        compiler_params=_params("parallel", "parallel"),
        name="mod_proj",
    )(cond, mod_w, mod_b.reshape(depth, 1, n_out))


def _ffn_body(final, with_pos, x_ref, g_ref, sh_ref, sc_ref, gt_ref, wg_ref, wu_ref, wo_ref, *rest):
    rest = list(rest)
    pos_ref = rest.pop(0) if with_pos else None
    fg_ref = rest.pop(0) if final else None
    o_ref, h_ref = rest
    j = pl.program_id(1)

    @pl.when(j == 0)
    def _():
        if with_pos:
            o_ref[...] = x_ref[...] + pos_ref[...]
            _norm_mod_into(h_ref, o_ref, g_ref, sh_ref, sc_ref)
        else:
            _norm_mod_into(h_ref, x_ref, g_ref, sh_ref, sc_ref)
            o_ref[...] = x_ref[...]

    h = h_ref[...]
    gate = _dot(h, wg_ref[...])
    up = _dot(h, wu_ref[...])
    a = (_silu(gate) * up).astype(BF16)
    o_ref[...] += (0.5 * gt_ref[...]) * _dot(a, wo_ref[...])

    if final:
        @pl.when(j == pl.num_programs(1) - 1)
        def _():
            _rmsnorm_rows(o_ref, fg_ref)


def _rmsnorm_rows(o_ref, g_ref):
    g = g_ref[...]

    def body(r, carry):
        rows = pl.ds(pl.multiple_of(r * NORM_ROWS, NORM_ROWS), NORM_ROWS)
        y = o_ref[rows, :]
        ms = jnp.mean(y * y, axis=-1, keepdims=True)
        o_ref[rows, :] = y * lax.rsqrt(ms + EPS) * g
        return carry

    lax.fori_loop(0, o_ref.shape[0] // NORM_ROWS, body, 0, unroll=NORM_UNROLL)


def _ffn_call(x, g, shift, scale, gate, w_in, w_out, layer, rows_per_mod, final_g=None, pos=None):
    t, d = x.shape
    d_ff = w_out.shape[1]
    tm = _tile(rows_per_mod, FFN_TM // 2 if pos is not None else FFN_TM)
    tf = _tile(d_ff, FFN_TF)
    nf = d_ff // tf
    final = final_g is not None
    with_pos = pos is not None
    tiles_per_mod = rows_per_mod // tm
    mod_spec = pl.BlockSpec((None, 1, d), lambda i, j: (i // tiles_per_mod, 0, 0))
    vec_spec = pl.BlockSpec((1, d), lambda i, j: (0, 0))
    in_specs = [
        pl.BlockSpec((tm, d), lambda i, j: (i, 0)),
        vec_spec, mod_spec, mod_spec, mod_spec,
        pl.BlockSpec((None, d, tf), lambda i, j: (layer, 0, j)),
        pl.BlockSpec((None, d, tf), lambda i, j: (layer, 0, j + nf)),
        pl.BlockSpec((None, tf, d), lambda i, j: (layer, j, 0)),
    ]
    args = [x, g, shift, scale, gate, w_in, w_in, w_out]
    if with_pos:
        in_specs.append(pl.BlockSpec((tm, d), lambda i, j: (i % tiles_per_mod, 0)))
        args.append(pos)
    if final:
        in_specs.append(vec_spec)
        args.append(final_g)
    return pl.pallas_call(
        functools.partial(_ffn_body, final, with_pos),
        grid=(t // tm, nf),
        in_specs=in_specs,
        out_specs=pl.BlockSpec((tm, d), lambda i, j: (i, 0)),
        out_shape=jax.ShapeDtypeStruct((t, d), F32),
        scratch_shapes=[pltpu.VMEM((tm, d), BF16)],
        compiler_params=_params("parallel", "arbitrary"),
        name="ffn",
    )(*args)


def _inproj_body(x_ref, g_ref, sh_ref, sc_ref, w_ref, o_ref, h_ref):
    @pl.when(pl.program_id(1) == 0)
    def _():
        _norm_mod_into(h_ref, x_ref, g_ref, sh_ref, sc_ref)

    o_ref[...] = _dot(h_ref[...], w_ref[...]).astype(BF16)


def _inproj_call(x, g, shift, scale, w, layer, rows_per_mod):
    t, d = x.shape
    n_out = w.shape[2]
    tm = _tile(rows_per_mod, 1024)
    tn = _tile(n_out, 1024)
    mod_spec = pl.BlockSpec((None, 1, d), lambda i, j: ((i * tm) // rows_per_mod, 0, 0))
    return pl.pallas_call(
        _inproj_body,
        grid=(t // tm, n_out // tn),
        in_specs=[
            pl.BlockSpec((tm, d), lambda i, j: (i, 0)),
            pl.BlockSpec((1, d), lambda i, j: (0, 0)),
            mod_spec, mod_spec,
            pl.BlockSpec((None, d, tn), lambda i, j: (layer, 0, j)),
        ],
        out_specs=pl.BlockSpec((tm, tn), lambda i, j: (i, j)),
        out_shape=jax.ShapeDtypeStruct((t, n_out), BF16),
        scratch_shapes=[pltpu.VMEM((tm, d), BF16)],
        compiler_params=_params("parallel", "arbitrary"),
        name="mix_in_proj",
    )(x, g, shift, scale, w)


def _outproj_body(x_ref, ya_ref, yb_ref, wa_ref, wb_ref, gt_ref, o_ref):
    y = _dot(ya_ref[...], wa_ref[...]) + _dot(yb_ref[...], wb_ref[...])
    o_ref[...] = x_ref[...] + gt_ref[...] * y


def _outproj_call(x, ya, yb, w, layer, gate, rows_per_mod):
    t, d = x.shape
    half = ya.shape[1]
    tm = _tile(rows_per_mod, 512)
    return pl.pallas_call(
        _outproj_body,
        grid=(t // tm,),
        in_specs=[
            pl.BlockSpec((tm, d), lambda i: (i, 0)),
            pl.BlockSpec((tm, half), lambda i: (i, 0)),
            pl.BlockSpec((tm, half), lambda i: (i, 0)),
            pl.BlockSpec((None, half, d), lambda i: (layer, 0, 0)),
            pl.BlockSpec((None, half, d), lambda i: (layer, 1, 0)),
            pl.BlockSpec((None, 1, d), lambda i: ((i * tm) // rows_per_mod, 0, 0)),
        ],
        out_specs=pl.BlockSpec((tm, d), lambda i: (i, 0)),
        out_shape=jax.ShapeDtypeStruct((t, d), F32),
        compiler_params=_params("parallel"),
        name="mix_out_proj",
    )(x, ya, yb, w, w, gate)


def _fnet_chan_body(gc, x_ref, cs_ref, z_ref):
    half = N_GROUPS * gc
    for g in range(N_GROUPS):
        xg = x_ref[:, g * gc:(g + 1) * gc].astype(BF16)
        z = _dot(xg, cs_ref[...])
        z_ref[:, g * gc:(g + 1) * gc] = z[:, :gc].astype(BF16)
        z_ref[:, half + g * gc:half + (g + 1) * gc] = z[:, gc:].astype(BF16)


def _fnet_chan_call(u3, cs, width):
    b, n, _ = u3.shape
    gc = width // N_GROUPS
    tr = _tile(n, 512)
    return pl.pallas_call(
        functools.partial(_fnet_chan_body, gc),
        grid=(b, n // tr),
        in_specs=[
            pl.BlockSpec((None, tr, width), lambda bi, i: (bi, i, 0)),
            pl.BlockSpec((gc, 2 * gc), lambda bi, i: (0, 0)),
        ],
        out_specs=pl.BlockSpec((None, tr, 2 * width), lambda bi, i: (bi, i, 0)),
        out_shape=jax.ShapeDtypeStruct((b, n, 2 * width), BF16),
        compiler_params=_params("parallel", "parallel"),
        name="fnet_chan_dft",
    )(u3, cs)


def _fnet_seq_body(gc, scale, c_ref, s_ref, z_ref, w_ref, o_ref, acc_ref):
    k = pl.program_id(2)
    half = N_GROUPS * gc

    @pl.when(k == 0)
    def _():
        acc_ref[...] = jnp.zeros_like(acc_ref)

    acc_ref[...] += _dot(c_ref[...], z_ref[:, :half]) - _dot(s_ref[...], z_ref[:, half:])

    @pl.when(k == pl.num_programs(2) - 1)
    def _():
        for g in range(N_GROUPS):
            r = (acc_ref[:, g * gc:(g + 1) * gc] * scale).astype(BF16)
            o_ref[:, g * gc:(g + 1) * gc] = _dot(r, w_ref[g]).astype(BF16)


def _fnet_seq_call(cn, sn, z, w):
    b, n, two_w = z.shape
    width = two_w // 2
    gc = width // N_GROUPS
    tm = _tile(n, 1024)
    tk = _tile(n, 512)
    scale = 1.0 / math.sqrt(n * gc)
    return pl.pallas_call(
        functools.partial(_fnet_seq_body, gc, scale),
        grid=(b, n // tm, n // tk),
        in_specs=[
            pl.BlockSpec((tm, tk), lambda bi, i, k: (i, k)),
            pl.BlockSpec((tm, tk), lambda bi, i, k: (i, k)),
            pl.BlockSpec((None, tk, two_w), lambda bi, i, k: (bi, k, 0)),
            pl.BlockSpec((N_GROUPS, gc, gc), lambda bi, i, k: (0, 0, 0)),
        ],
        out_specs=pl.BlockSpec((None, tm, width), lambda bi, i, k: (bi, i, 0)),
        out_shape=jax.ShapeDtypeStruct((b, n, width), BF16),
        scratch_shapes=[pltpu.VMEM((tm, width), F32)],
        compiler_params=_params("parallel", "parallel", "arbitrary"),
        name="fnet_seq_dft",
    )(cn, sn, z, w)


def _hy_pre_body(n, u0_ref, u1_ref, u2_ref, w0_ref, w1_ref, w2_ref, b0_ref, b1_ref, b2_ref,
                 x0_ref, vv_ref, vn_ref):
    row = lax.broadcasted_iota(jnp.int32, u0_ref.shape, 0)

    def conv(u_ref, w_ref, b_ref):
        u = u_ref[...].astype(F32)
        prev = jnp.where(row == 0, 0.0, pltpu.roll(u, 1, 0))
        nxt = jnp.where(row == n - 1, 0.0, pltpu.roll(u, n - 1, 0))
        return prev * w_ref[0:1, :] + u * w_ref[1:2, :] + nxt * w_ref[2:3, :] + b_ref[...]

    x0_ref[...] = conv(u0_ref, w0_ref, b0_ref).astype(BF16)
    vv = conv(u2_ref, w2_ref, b2_ref) * conv(u1_ref, w1_ref, b1_ref)
    vv_ref[...] = vv.astype(BF16)
    vn_ref[...] = jnp.sum(jnp.where(row % 2 == 0, vv, -vv), axis=0, keepdims=True)


def _hy_pre_call(u3, col0, width, conv_w, conv_b):
    b, n, _ = u3.shape
    tc = _tile(width, LANE)
    nct = width // tc

    def u_spec(s):
        return pl.BlockSpec((None, n, tc), lambda bi, c: (bi, 0, (col0 + s * width) // tc + c))

    def w_spec(s):
        return pl.BlockSpec((3, tc), lambda bi, c: (0, s * nct + c))

    def b_spec(s):
        return pl.BlockSpec((1, tc), lambda bi, c: (0, s * nct + c))

    out3 = pl.BlockSpec((None, n, tc), lambda bi, c: (bi, 0, c))
    return pl.pallas_call(
        functools.partial(_hy_pre_body, n),
        grid=(b, nct),
        in_specs=[u_spec(0), u_spec(1), u_spec(2), w_spec(0), w_spec(1), w_spec(2),
                  b_spec(0), b_spec(1), b_spec(2)],
        out_specs=[out3, out3, pl.BlockSpec((None, 1, tc), lambda bi, c: (bi, 0, c))],
        out_shape=[jax.ShapeDtypeStruct((b, n, width), BF16),
                   jax.ShapeDtypeStruct((b, n, width), BF16),
                   jax.ShapeDtypeStruct((b, 1, width), F32)],
        compiler_params=_params("parallel", "parallel"),
        name="hyena_short_conv",
    )(u3, u3, u3, conv_w, conv_w, conv_w, conv_b, conv_b, conv_b)


def _hy_filter_body(z_ref, w1_ref, b1_ref, w2_ref, b2_ref, fr_ref, wf_ref, wb_ref, dec_ref,
                    hs_ref, hd_ref, tn_ref):
    hp = lax.Precision.HIGHEST
    z = z_ref[...]
    fr = fr_ref[...]
    h = jnp.sin(fr * (jnp.dot(z, w1_ref[...], precision=hp, preferred_element_type=F32) + b1_ref[...]))
    h = jnp.sin(fr * (jnp.dot(h, w2_ref[...], precision=hp, preferred_element_type=F32) + b2_ref[...]))
    window = jnp.exp(-z[:, 0:1] * jnp.abs(dec_ref[...]))
    hf = jnp.dot(h, wf_ref[...], precision=hp, preferred_element_type=F32) * window
    hb = jnp.dot(h, wb_ref[...], precision=hp, preferred_element_type=F32) * window
    row = lax.broadcasted_iota(jnp.int32, hf.shape, 0)
    hb = jnp.where(row == 0, 0.0, hb)
    l1 = jnp.sum(jnp.abs(hf), axis=0, keepdims=True) + jnp.sum(jnp.abs(hb), axis=0, keepdims=True)
    hs = (hf + hb) / l1
    hd = (hb - hf) / l1
    hs_ref[...] = hs.astype(BF16)
    hd_ref[...] = hd.astype(BF16)
    tn_ref[...] = jnp.sum(jnp.where(row % 2 == 0, hs, -hs), axis=0, keepdims=True)


def _hy_filter_call(z, w1, b1, w2, b2, freq, w_out, decay):
    n, kz = z.shape
    hid = w1.shape[1]
    width = decay.shape[1]
    tc = _tile(width, 256)
    nct = width // tc
    full = lambda a: pl.BlockSpec(a.shape, lambda c: (0, 0))
    col = pl.BlockSpec((n, tc), lambda c: (0, c))
    vec = pl.BlockSpec((1, tc), lambda c: (0, c))
    return pl.pallas_call(
        _hy_filter_body,
        grid=(nct,),
        in_specs=[full(z), full(w1), full(b1), full(w2), full(b2), full(freq),
                  pl.BlockSpec((hid, tc), lambda c: (0, c)),
                  pl.BlockSpec((hid, tc), lambda c: (0, c + nct)),
                  vec],
        out_specs=[col, col, vec],
        out_shape=[jax.ShapeDtypeStruct((n, width), BF16),
                   jax.ShapeDtypeStruct((n, width), BF16),
                   jax.ShapeDtypeStruct((1, width), F32)],
        compiler_params=_params("parallel"),
        name="hyena_filter",
    )(z, w1, b1, w2, b2, freq, w_out, w_out, decay)


def _hy_spec_body(c_ref, s_ref, hs_ref, hd_ref, tre_ref, tim_ref):
    @pl.when(pl.program_id(1) == 0)
    def _():
        tre_ref[...] = jnp.zeros_like(tre_ref)
        tim_ref[...] = jnp.zeros_like(tim_ref)

    tre_ref[...] += _dot(c_ref[...], hs_ref[...])
    tim_ref[...] += _dot(s_ref[...], hd_ref[...])


def _hy_spec_call(cq, sq, hs, hd):
    n, width = hs.shape
    tm = _tile(n, 512)
    tk = _tile(n, 512)
    mat = pl.BlockSpec((tm, tk), lambda i, k: (i, k))
    rhs = pl.BlockSpec((tk, width), lambda i, k: (k, 0))
    out = pl.BlockSpec((tm, width), lambda i, k: (i, 0))
    return pl.pallas_call(
        _hy_spec_body,
        grid=(n // tm, n // tk),
        in_specs=[mat, mat, rhs, rhs],
        out_specs=[out, out],
        out_shape=[jax.ShapeDtypeStruct((n, width), F32)] * 2,
        compiler_params=_params("parallel", "arbitrary"),
        name="hyena_filter_spectrum",
    )(cq, sq, hs, hd)


def _hy_fwd_body(n, width, c_ref, s_ref, v_ref, tre_ref, tim_ref, p_ref, ar_ref, ai_ref):
    k = pl.program_id(2)

    @pl.when(k == 0)
    def _():
        ar_ref[...] = jnp.zeros_like(ar_ref)
        ai_ref[...] = jnp.zeros_like(ai_ref)

    v = v_ref[...]
    ar_ref[...] += _dot(c_ref[...], v)
    ai_ref[...] += _dot(s_ref[...], v)

    @pl.when(k == pl.num_programs(2) - 1)
    def _():
        tm = ar_ref.shape[0]
        freq = lax.broadcasted_iota(jnp.int32, (tm, 1), 0) + pl.program_id(1) * tm
        wk = jnp.where(freq == 0, 0.5 / n, 1.0 / n)
        ar, ai, tre, tim = ar_ref[...], ai_ref[...], tre_ref[...], tim_ref[...]
        p_ref[:, :width] = ((ar * tre + ai * tim) * wk).astype(BF16)
        p_ref[:, width:] = ((ar * tim - ai * tre) * wk).astype(BF16)


def _hy_fwd_call(cq, sq, vv, tre, tim):
    b, n, width = vv.shape
    tm = _tile(n, 1024)
    tk = _tile(n, 512)
    mat = pl.BlockSpec((tm, tk), lambda bi, i, k: (i, k))
    spec = pl.BlockSpec((tm, width), lambda bi, i, k: (i, 0))
    return pl.pallas_call(
        functools.partial(_hy_fwd_body, n, width),
        grid=(b, n // tm, n // tk),
        in_specs=[mat, mat, pl.BlockSpec((None, tk, width), lambda bi, i, k: (bi, k, 0)), spec, spec],
        out_specs=pl.BlockSpec((None, tm, 2 * width), lambda bi, i, k: (bi, i, 0)),
        out_shape=jax.ShapeDtypeStruct((b, n, 2 * width), BF16),
        scratch_shapes=[pltpu.VMEM((tm, width), F32), pltpu.VMEM((tm, width), F32)],
        compiler_params=_params("parallel", "parallel", "arbitrary"),
        name="hyena_fwd_dft",
    )(cq, sq, vv, tre, tim)


def _hy_inv_body(n, width, c_ref, s_ref, p_ref, x0_ref, vv_ref, vn_ref, tn_ref, bias_ref, o_ref, acc_ref):
    k = pl.program_id(2)

    @pl.when(k == 0)
    def _():
        acc_ref[...] = jnp.zeros_like(acc_ref)

    acc_ref[...] += _dot(c_ref[...], p_ref[:, :width]) - _dot(s_ref[...], p_ref[:, width:])

    @pl.when(k == pl.num_programs(2) - 1)
    def _():
        tm = acc_ref.shape[0]
        pos = lax.broadcasted_iota(jnp.int32, (tm, 1), 0) + pl.program_id(1) * tm
        sign = jnp.where(pos % 2 == 0, 1.0, -1.0)
        nyq = (vn_ref[...] * tn_ref[...]) * (0.5 / n)
        conv = acc_ref[...] + sign * nyq
        o_ref[...] = (x0_ref[...].astype(F32) * (conv + vv_ref[...].astype(F32) * bias_ref[...])).astype(BF16)


def _hy_inv_call(cq, sq, p, x0, vv, vn, tn, bias):
    b, n, width = vv.shape
    tm = _tile(n, 1024)
    tk = _tile(n, 512)
    mat = pl.BlockSpec((tm, tk), lambda bi, i, k: (i, k))
    tok = pl.BlockSpec((None, tm, width), lambda bi, i, k: (bi, i, 0))
    vec = pl.BlockSpec((1, width), lambda bi, i, k: (0, 0))
    return pl.pallas_call(
        functools.partial(_hy_inv_body, n, width),
        grid=(b, n // tm, n // tk),
        in_specs=[mat, mat, pl.BlockSpec((None, tk, 2 * width), lambda bi, i, k: (bi, k, 0)),
                  tok, tok, pl.BlockSpec((None, 1, width), lambda bi, i, k: (bi, 0, 0)), vec, vec],
        out_specs=tok,
        out_shape=jax.ShapeDtypeStruct((b, n, width), BF16),
        scratch_shapes=[pltpu.VMEM((tm, width), F32)],
        compiler_params=_params("parallel", "parallel", "arbitrary"),
        name="hyena_inv_dft",
    )(cq, sq, p, x0, vv, vn, tn, bias)


def _ret_body(dk, ch, nc, lgf_ref, lgb_ref, q_ref, k_ref, v_ref, g_ref, gn_ref, s0f_ref, s0b_ref,
              y_ref, sfo_ref, sbo_ref, o_ref, sf_ref, sb_ref):
    h = pl.program_id(1)
    sf_ref[...] = s0f_ref[...]
    sb_ref[...] = s0b_ref[...]

    ii = lax.broadcasted_iota(jnp.int32, (ch, ch), 0)
    jj = lax.broadcasted_iota(jnp.int32, (ch, ch), 1)
    diff = (ii - jj).astype(F32)
    idx = lax.broadcasted_iota(jnp.int32, (ch, 1), 0).astype(F32)
    k_scale = dk ** -0.5

    def decays(lg, lag, q_pow, k_pow):
        inner = jnp.where(lag >= 0, jnp.exp(lg * jnp.maximum(lag, 0.0)), 0.0)
        return inner, jnp.exp(lg * q_pow), jnp.exp(lg * k_pow), jnp.exp(lg * jnp.full((1, 1), float(ch), F32))

    dec_f = decays(lgf_ref[h], diff, idx + 1.0, (ch - 1.0) - idx)
    dec_b = decays(lgb_ref[h], -diff, float(ch) - idx, idx)

    def rows(c):
        return pl.ds(pl.multiple_of(c * ch, ch), ch)

    def chunk(c, dec, s_ref):
        inner_decay, q_decay, k_decay, chunk_decay = dec
        q = q_ref[rows(c), :]
        k = k_ref[rows(c), :].astype(F32) * k_scale
        v = v_ref[rows(c), :]
        scores = lax.dot_general(q, k.astype(BF16), (((1,), (1,)), ((), ())),
                                 preferred_element_type=F32) * inner_decay
        s = s_ref[...]
        out = _dot(scores.astype(BF16), v) + _dot(q, s.astype(BF16)) * q_decay
        kd = (k * k_decay).T.astype(BF16)
        s_ref[...] = s * chunk_decay + _dot(kd, v)
        return out

    def finish(c, o):
        mu = jnp.mean(o, axis=-1, keepdims=True)
        var = jnp.mean(jnp.square(o - mu), axis=-1, keepdims=True)
        on = (o - mu) * lax.rsqrt(var + EPS) * gn_ref[...]
        y_ref[rows(c), :] = (_silu(g_ref[rows(c), :].astype(F32)) * on).astype(BF16)

    def first_half(c, carry):
        cb = nc - 1 - c
        o_ref[rows(c), :] = chunk(c, dec_f, sf_ref)
        o_ref[rows(cb), :] = chunk(cb, dec_b, sb_ref)
        return carry

    def second_half(c, carry):
        cb = nc - 1 - c
        finish(c, o_ref[rows(c), :] + chunk(c, dec_f, sf_ref))
        finish(cb, o_ref[rows(cb), :] + chunk(cb, dec_b, sb_ref))
        return carry

    lax.fori_loop(0, nc // 2, first_half, 0)
    lax.fori_loop(nc // 2, nc, second_half, 0)
    sfo_ref[...] = sf_ref[...]
    sbo_ref[...] = sb_ref[...]


def _ret_call(u3, width, lgf, lgb, gn, s0f, s0b):
    b, n, _ = u3.shape
    dk = width // N_GROUPS
    ch = min(RET_CHUNK, n // 2)
    nc = n // ch
    assert nc % 2 == 0 and nc * ch == n

    def tok(part):
        return pl.BlockSpec((None, n, dk), lambda bi, h: (bi, 0, part * N_GROUPS + h))

    state = pl.BlockSpec((None, None, dk, dk), lambda bi, h: (bi, h, 0, 0))
    smem = pl.BlockSpec(memory_space=pltpu.SMEM)
    return pl.pallas_call(
        functools.partial(_ret_body, dk, ch, nc),
        grid=(b, N_GROUPS),
        in_specs=[smem, smem, tok(0), tok(1), tok(2), tok(3),
                  pl.BlockSpec((1, dk), lambda bi, h: (0, h)), state, state],
        out_specs=[pl.BlockSpec((None, n, dk), lambda bi, h: (bi, 0, h)), state, state],
        out_shape=[jax.ShapeDtypeStruct((b, n, width), BF16),
                   jax.ShapeDtypeStruct((b, N_GROUPS, dk, dk), F32),
                   jax.ShapeDtypeStruct((b, N_GROUPS, dk, dk), F32)],
        scratch_shapes=[pltpu.VMEM((n, dk), F32), pltpu.VMEM((dk, dk), F32), pltpu.VMEM((dk, dk), F32)],
        compiler_params=_params("parallel", "parallel"),
        name="retention",
    )(lgf, lgb, u3, u3, u3, u3, gn, s0f, s0b)


def _pool_body(n, x_ref, w_ref, sc_ref, y_ref):
    g = pl.program_id(1)
    gc = x_ref.shape[1]
    n_pad = n + 2 * POOL_PAD
    pos = lax.broadcasted_iota(jnp.int32, (n, 1), 0)

    for gi, win in enumerate(POOL_WINDOWS):
        @pl.when(g == gi)
        def _(win=win):
            x = x_ref[...].astype(F32)
            zeros = jnp.zeros((POOL_PAD, gc), F32)
            s = jnp.concatenate([zeros, x, zeros], axis=0)
            span = 1
            while span < win:
                s = s + pltpu.roll(s, span, 0)
                span *= 2
            total = pltpu.roll(s, n_pad - (POOL_PAD + win // 2 - 1), 0)[:n]
            lo = jnp.clip(pos - win // 2, 0, n - 1)
            hi = jnp.clip(pos - win // 2 + win - 1, 0, n - 1)
            count = (hi - lo + 1).astype(F32)
            p = total / count - x
            y = _dot(p.astype(BF16), w_ref[...]) * sc_ref[...]
            y_ref[...] = y.astype(BF16)


def _pool_call(u3, col0, width, w, scale):
    b, n, _ = u3.shape
    gc = width // N_GROUPS
    assert max(POOL_WINDOWS) // 2 <= POOL_PAD
    return pl.pallas_call(
        functools.partial(_pool_body, n),
        grid=(b, N_GROUPS),
        in_specs=[pl.BlockSpec((None, n, gc), lambda bi, g: (bi, 0, col0 // gc + g)),
                  pl.BlockSpec((None, gc, gc), lambda bi, g: (g, 0, 0)),
                  pl.BlockSpec((1, gc), lambda bi, g: (0, g))],
        out_specs=pl.BlockSpec((None, n, gc), lambda bi, g: (bi, 0, g)),
        out_shape=jax.ShapeDtypeStruct((b, n, width), BF16),
        compiler_params=_params("parallel", "parallel"),
        name="pool_mix",
    )(u3, w, scale)


def _dft_tables(rows, cols, period):
    assert period % 8 == 0
    k = jnp.arange(rows, dtype=jnp.int32)
    s = jnp.arange(cols, dtype=jnp.int32)
    m = (k[:, None] * s[None, :]) % period
    quarter = period // 4
    quad = m // quarter
    r = m % quarter
    swap = r > quarter // 2
    x = jnp.where(swap, quarter - r, r).astype(F32) * (2.0 * math.pi / period)
    x2 = x * x
    sin_x = x * (1.0 + x2 * (-1.0 / 6 + x2 * (1.0 / 120 + x2 * (-1.0 / 5040 + x2 * (1.0 / 362880)))))
    cos_x = 1.0 + x2 * (-0.5 + x2 * (1.0 / 24 + x2 * (-1.0 / 720 + x2 * (1.0 / 40320 - x2 * (1.0 / 3628800)))))
    c0 = jnp.where(swap, sin_x, cos_x)
    s0 = jnp.where(swap, cos_x, sin_x)
    cos_t = jnp.where(quad == 0, c0, jnp.where(quad == 1, -s0, jnp.where(quad == 2, -c0, s0)))
    sin_t = jnp.where(quad == 0, s0, jnp.where(quad == 1, c0, jnp.where(quad == 2, -s0, -c0)))
    return cos_t.astype(BF16), sin_t.astype(BF16)


def _fnet_tables(n):
    top_c, top_s = _dft_tables(n // 2, n, n)
    sign = jnp.where(jnp.arange(n) % 2 == 0, 1.0, -1.0).astype(BF16)[None, :]
    return jnp.concatenate([top_c, top_c * sign], axis=0), jnp.concatenate([top_s, top_s * sign], axis=0)


def _grid_pos_embed(n_tokens, d):
    rows = n_tokens // GRID_W
    rr, cc = jnp.meshgrid(jnp.arange(rows, dtype=F32), jnp.arange(GRID_W, dtype=F32), indexing='ij')
    rr = rr.reshape(-1)[:, None]
    cc = cc.reshape(-1)[:, None]
    quarter = d // 4
    omega = 1.0 / (POS_BASE ** (jnp.arange(quarter, dtype=F32) / quarter))
    ar, ac = rr * omega, cc * omega
    return jnp.concatenate([jnp.sin(ar), jnp.cos(ar), jnp.sin(ac), jnp.cos(ac)], axis=-1)


def _hyena_pos_features(n):
    t_idx = jnp.arange(n, dtype=F32)
    t = jnp.linspace(0.0, 1.0, n, dtype=F32)
    bands = jnp.linspace(1e-4, HY_BANDS - 1, HY_BANDS, dtype=F32)
    ang = (2.0 * math.pi / n) * t_idx[:, None] * bands[None, :]
    z = jnp.concatenate([t[:, None], jnp.cos(ang), jnp.sin(ang)], axis=-1)
    return jnp.pad(z, ((0, 0), (0, LANE - z.shape[1])))


def _pad2(a, rows, cols):
    return jnp.pad(a, ((0, rows - a.shape[0]), (0, cols - a.shape[1])))


def kernel(x_prompt, x_sample, state_ret_fwd, state_ret_bwd, c, c_ctx, norm_g, mod_w, mod_b, ffn_a_in, ffn_a_out, ffn_b_in, ffn_b_out, ev_in_w, ev_out_w, fnet_w, hy_conv_w, hy_conv_b, hy_w1, hy_b1, hy_w2, hy_b2, hy_w_out, hy_freq, hy_decay, hy_bias, od_in_w, od_out_w, ret_log_decay_fwd, ret_log_decay_bwd, ret_gn, pool_w, pool_scale, final_norm):
    depth, d = norm_g.shape[0], norm_g.shape[2]
    half = d // 2
    dec_b = x_sample.shape[0]
    assert dec_b + 1 <= MOD_ROWS

    cond = jnp.concatenate([c, c_ctx[None, :]], axis=0)
    cond = jnp.pad(cond, ((0, MOD_ROWS - cond.shape[0]), (0, 0)))
    mod = _mod_call(cond, mod_w, mod_b).reshape(depth, MOD_ROWS, N_MOD, 1, d)

    bf = lambda a: a.astype(BF16)
    ffn_a_in, ffn_a_out, ffn_b_in, ffn_b_out = bf(ffn_a_in), bf(ffn_a_out), bf(ffn_b_in), bf(ffn_b_out)
    ev_in_w, ev_out_w, od_in_w, od_out_w = bf(ev_in_w), bf(ev_out_w), bf(od_in_w), bf(od_out_w)
    fnet_w, pool_w = bf(fnet_w), bf(pool_w)

    hid = hy_w1.shape[2]
    n_even = hy_w1.shape[0]
    hy_w1p = jnp.stack([_pad2(hy_w1[e], LANE, LANE) for e in range(n_even)])
    hy_w2p = jnp.stack([_pad2(hy_w2[e], LANE, LANE) for e in range(n_even)])
    hy_woutp = jnp.pad(hy_w_out, ((0, 0), (0, LANE - hid), (0, 0)))
    padv = lambda a: jnp.pad(a, ((0, 0), (0, LANE - hid)))[:, None, :]
    hy_b1p, hy_b2p, hy_freqp = padv(hy_b1), padv(hy_b2), padv(hy_freq)

    gc = half // N_GROUPS
    cs_chan = jnp.concatenate(_dft_tables(gc, gc, gc), axis=1)

    def trunk(x3, mod_rows, s0_f, s0_b, pos=None):
        b, n, _ = x3.shape
        t = b * n
        rows_per_mod = n if mod_rows.stop - mod_rows.start > 1 else t
        x = x3.reshape(t, d)
        cq, sq = _dft_tables(n, n, 2 * n)
        cn, sn = _fnet_tables(n)
        z_pos = _hyena_pos_features(n)
        s_f_out, s_b_out = [], []
        for l in range(depth):
            m = [mod[l, mod_rows, i] for i in range(N_MOD)]
            x = _ffn_call(x, norm_g[l, 0][None], m[0], m[1], m[2], ffn_a_in, ffn_a_out, l, rows_per_mod,
                          pos=pos if l == 0 else None)
            if l % 2 == 0:
                e = l // 2
                u3 = _inproj_call(x, norm_g[l, 1][None], m[3], m[4], ev_in_w, e, rows_per_mod).reshape(b, n, -1)
                zc = _fnet_chan_call(u3, cs_chan, half)
                ya = _fnet_seq_call(cn, sn, zc, fnet_w[e])
                x0, vv, vn = _hy_pre_call(u3, half, half, hy_conv_w[e], hy_conv_b[e][None])
                hs, hd, tn = _hy_filter_call(z_pos, hy_w1p[e], hy_b1p[e], hy_w2p[e], hy_b2p[e], hy_freqp[e],
                                             hy_woutp[e], hy_decay[e][None])
                tre, tim = _hy_spec_call(cq, sq, hs, hd)
                p = _hy_fwd_call(cq, sq, vv, tre, tim)
                yb = _hy_inv_call(cq, sq, p, x0, vv, vn, tn, hy_bias[e][None])
                w_out, w_idx = ev_out_w, e
            else:
                o = l // 2
                u3 = _inproj_call(x, norm_g[l, 1][None], m[3], m[4], od_in_w, o, rows_per_mod).reshape(b, n, -1)
                ya, s_f, s_b = _ret_call(u3, half, ret_log_decay_fwd[o], ret_log_decay_bwd[o], ret_gn[o][None],
                                         s0_f[:, o], s0_b[:, o])
                yb = _pool_call(u3, 4 * half, half, pool_w[o], pool_scale[o][None])
                s_f_out.append(s_f)
                s_b_out.append(s_b)
                w_out, w_idx = od_out_w, o
            x = _outproj_call(x, ya.reshape(t, half), yb.reshape(t, half), w_out, w_idx, m[5], rows_per_mod)
            x = _ffn_call(x, norm_g[l, 2][None], m[6], m[7], m[8], ffn_b_in, ffn_b_out, l, rows_per_mod,
                          final_g=final_norm[None] if l == depth - 1 else None)
        return x.reshape(b, n, d), jnp.stack(s_f_out, axis=1), jnp.stack(s_b_out, axis=1)

    zero_state = jnp.zeros((x_prompt.shape[0],) + state_ret_fwd.shape[1:], F32)
    y_prompt, st_f, st_b = trunk(x_prompt, slice(dec_b, dec_b + 1), zero_state, zero_state)

    y_sample, _, _ = trunk(x_sample, slice(0, dec_b), state_ret_fwd, state_ret_bwd,
                           pos=_grid_pos_embed(x_sample.shape[1], d))
    return (y_prompt, y_sample, st_f, st_b)
```

```python
import functools
import math

import jax
import jax.numpy as jnp
from jax import lax
from jax.experimental import pallas as pl
from jax.experimental.pallas import tpu as pltpu

F32 = jnp.float32
BF16 = jnp.bfloat16

EPS = 1e-6
N_MOD = 9
GRID_W = 64
POS_BASE = 10000.0
N_GROUPS = 4
HY_BANDS = 16
RET_CHUNK = 256
POOL_WINDOWS = (2, 4, 8, 16)
POOL_PAD = 8
LANE = 128
MOD_ROWS = 16
VMEM_LIMIT = 56 * 1024 * 1024
NORM_ROWS = 32
NORM_UNROLL = 4
FFN_TM = 1024
FFN_TF = 512


def _params(*sem):
    return pltpu.CompilerParams(dimension_semantics=sem, vmem_limit_bytes=VMEM_LIMIT)


def _tile(n, pref):
    if n <= pref:
        return n
    t = pref
    while n % t:
        t -= LANE
    assert t > 0, (n, pref)
    return t


def _silu(x):
    return x * (1.0 / (1.0 + jnp.exp(-x)))


def _norm_mod_into(h_ref, x_ref, g_ref, shift_ref, scale_ref):
    gs = g_ref[...] * (1.0 + scale_ref[...])
    shift = shift_ref[...]

    def body(r, carry):
        rows = pl.ds(pl.multiple_of(r * NORM_ROWS, NORM_ROWS), NORM_ROWS)
        x = x_ref[rows, :]
        ms = jnp.mean(x * x, axis=-1, keepdims=True)
        h_ref[rows, :] = (x * lax.rsqrt(ms + EPS) * gs + shift).astype(h_ref.dtype)
        return carry

    lax.fori_loop(0, x_ref.shape[0] // NORM_ROWS, body, 0, unroll=NORM_UNROLL)


def _dot(a, b):
    return jnp.dot(a, b, preferred_element_type=F32)


def _mod_body(c_ref, w_ref, b_ref, o_ref):
    s = _silu(c_ref[...]).astype(BF16)
    o_ref[...] = _dot(s, w_ref[...].astype(BF16)) + b_ref[...]


def _mod_call(cond, mod_w, mod_b):
    depth, d, n_out = mod_w.shape
    tn = _tile(n_out, 1024)
    return pl.pallas_call(
        _mod_body,
        grid=(depth, n_out // tn),
        in_specs=[
            pl.BlockSpec((MOD_ROWS, d), lambda l, j: (0, 0)),
            pl.BlockSpec((None, d, tn), lambda l, j: (l, 0, j)),
            pl.BlockSpec((None, 1, tn), lambda l, j: (l, 0, j)),
        ],
        out_specs=pl.BlockSpec((None, MOD_ROWS, tn), lambda l, j: (l, 0, j)),
        out_shape=jax.ShapeDtypeStruct((depth, MOD_ROWS, n_out), F32),
        compiler_params=_params("parallel", "parallel"),
        name="mod_proj",
    )(cond, mod_w, mod_b.reshape(depth, 1, n_out))


def _ffn_body(final, with_pos, x_ref, g_ref, sh_ref, sc_ref, gt_ref, wg_ref, wu_ref, wo_ref, *rest):
    rest = list(rest)
    pos_ref = rest.pop(0) if with_pos else None
    fg_ref = rest.pop(0) if final else None
    o_ref, h_ref = rest
    j = pl.program_id(1)

    @pl.when(j == 0)
    def _():
        if with_pos:
            o_ref[...] = x_ref[...] + pos_ref[...]
            _norm_mod_into(h_ref, o_ref, g_ref, sh_ref, sc_ref)
        else:
            _norm_mod_into(h_ref, x_ref, g_ref, sh_ref, sc_ref)
            o_ref[...] = x_ref[...]

    h = h_ref[...]
    gate = _dot(h, wg_ref[...])
    up = _dot(h, wu_ref[...])
    a = (_silu(gate) * up).astype(BF16)
    o_ref[...] += (0.5 * gt_ref[...]) * _dot(a, wo_ref[...])

    if final:
        @pl.when(j == pl.num_programs(1) - 1)
        def _():
            _rmsnorm_rows(o_ref, fg_ref)


def _rmsnorm_rows(o_ref, g_ref):
    g = g_ref[...]

    def body(r, carry):
        rows = pl.ds(pl.multiple_of(r * NORM_ROWS, NORM_ROWS), NORM_ROWS)
        y = o_ref[rows, :]
        ms = jnp.mean(y * y, axis=-1, keepdims=True)
        o_ref[rows, :] = y * lax.rsqrt(ms + EPS) * g
        return carry

    lax.fori_loop(0, o_ref.shape[0] // NORM_ROWS, body, 0, unroll=NORM_UNROLL)


def _ffn_call(x, g, shift, scale, gate, w_in, w_out, layer, rows_per_mod, final_g=None, pos=None):
    t, d = x.shape
    d_ff = w_out.shape[1]
    tm = _tile(rows_per_mod, FFN_TM // 2 if pos is not None else FFN_TM)
    tf = _tile(d_ff, FFN_TF)
    nf = d_ff // tf
    final = final_g is not None
    with_pos = pos is not None
    tiles_per_mod = rows_per_mod // tm
    mod_spec = pl.BlockSpec((None, 1, d), lambda i, j: (i // tiles_per_mod, 0, 0))
    vec_spec = pl.BlockSpec((1, d), lambda i, j: (0, 0))
    in_specs = [
        pl.BlockSpec((tm, d), lambda i, j: (i, 0)),
        vec_spec, mod_spec, mod_spec, mod_spec,
        pl.BlockSpec((None, d, tf), lambda i, j: (layer, 0, j)),
        pl.BlockSpec((None, d, tf), lambda i, j: (layer, 0, j + nf)),
        pl.BlockSpec((None, tf, d), lambda i, j: (layer, j, 0)),
    ]
    args = [x, g, shift, scale, gate, w_in, w_in, w_out]
    if with_pos:
        in_specs.append(pl.BlockSpec((tm, d), lambda i, j: (i % tiles_per_mod, 0)))
        args.append(pos)
    if final:
        in_specs.append(vec_spec)
        args.append(final_g)
    return pl.pallas_call(
        functools.partial(_ffn_body, final, with_pos),
        grid=(t // tm, nf),
        in_specs=in_specs,
        out_specs=pl.BlockSpec((tm, d), lambda i, j: (i, 0)),
        out_shape=jax.ShapeDtypeStruct((t, d), F32),
        scratch_shapes=[pltpu.VMEM((tm, d), BF16)],
        compiler_params=_params("parallel", "arbitrary"),
        name="ffn",
    )(*args)


def _inproj_body(x_ref, g_ref, sh_ref, sc_ref, w_ref, o_ref, h_ref):
    @pl.when(pl.program_id(1) == 0)
    def _():
        _norm_mod_into(h_ref, x_ref, g_ref, sh_ref, sc_ref)

    o_ref[...] = _dot(h_ref[...], w_ref[...]).astype(BF16)


def _inproj_call(x, g, shift, scale, w, layer, rows_per_mod):
    t, d = x.shape
    n_out = w.shape[2]
    tm = _tile(rows_per_mod, 1024)
    tn = _tile(n_out, 1024)
    mod_spec = pl.BlockSpec((None, 1, d), lambda i, j: ((i * tm) // rows_per_mod, 0, 0))
    return pl.pallas_call(
        _inproj_body,
        grid=(t // tm, n_out // tn),
        in_specs=[
            pl.BlockSpec((tm, d), lambda i, j: (i, 0)),
            pl.BlockSpec((1, d), lambda i, j: (0, 0)),
            mod_spec, mod_spec,
            pl.BlockSpec((None, d, tn), lambda i, j: (layer, 0, j)),
        ],
        out_specs=pl.BlockSpec((tm, tn), lambda i, j: (i, j)),
        out_shape=jax.ShapeDtypeStruct((t, n_out), BF16),
        scratch_shapes=[pltpu.VMEM((tm, d), BF16)],
        compiler_params=_params("parallel", "arbitrary"),
        name="mix_in_proj",
    )(x, g, shift, scale, w)


def _outproj_body(x_ref, ya_ref, yb_ref, wa_ref, wb_ref, gt_ref, o_ref):
    y = _dot(ya_ref[...], wa_ref[...]) + _dot(yb_ref[...], wb_ref[...])
    o_ref[...] = x_ref[...] + gt_ref[...] * y


def _outproj_call(x, ya, yb, w, layer, gate, rows_per_mod):
    t, d = x.shape
    half = ya.shape[1]
    tm = _tile(rows_per_mod, 512)
    return pl.pallas_call(
        _outproj_body,
        grid=(t // tm,),
        in_specs=[
            pl.BlockSpec((tm, d), lambda i: (i, 0)),
            pl.BlockSpec((tm, half), lambda i: (i, 0)),
            pl.BlockSpec((tm, half), lambda i: (i, 0)),
            pl.BlockSpec((None, half, d), lambda i: (layer, 0, 0)),
            pl.BlockSpec((None, half, d), lambda i: (layer, 1, 0)),
            pl.BlockSpec((None, 1, d), lambda i: ((i * tm) // rows_per_mod, 0, 0)),
        ],
        out_specs=pl.BlockSpec((tm, d), lambda i: (i, 0)),
        out_shape=jax.ShapeDtypeStruct((t, d), F32),
        compiler_params=_params("parallel"),
        name="mix_out_proj",
    )(x, ya, yb, w, w, gate)


def _fnet_chan_body(gc, x_ref, cs_ref, z_ref, za_ref):
    half = N_GROUPS * gc
    tr = x_ref.shape[0]
    sign = jnp.where(lax.broadcasted_iota(jnp.int32, (tr, 1), 0) % 2 == 0, 1.0, -1.0)

    @pl.when(pl.program_id(1) == 0)
    def _():
        za_ref[...] = jnp.zeros_like(za_ref)

    for g in range(N_GROUPS):
        xg = x_ref[:, g * gc:(g + 1) * gc].astype(BF16)
        z = _dot(xg, cs_ref[...])
        zc = z[:, :gc].astype(BF16)
        z_ref[:, g * gc:(g + 1) * gc] = zc
        z_ref[:, half + g * gc:half + (g + 1) * gc] = z[:, gc:].astype(BF16)
        za_ref[:, g * gc:(g + 1) * gc] += jnp.sum(zc.astype(F32) * sign, axis=0, keepdims=True)


def _fnet_chan_call(u3, cs, width):
    b, n, _ = u3.shape
    gc = width // N_GROUPS
    tr = _tile(n, 512)
    assert tr % 2 == 0
    return pl.pallas_call(
        functools.partial(_fnet_chan_body, gc),
        grid=(b, n // tr),
        in_specs=[
            pl.BlockSpec((None, tr, width), lambda bi, i: (bi, i, 0)),
            pl.BlockSpec((gc, 2 * gc), lambda bi, i: (0, 0)),
        ],
        out_specs=[pl.BlockSpec((None, tr, 2 * width), lambda bi, i: (bi, i, 0)),
                   pl.BlockSpec((None, 1, width), lambda bi, i: (bi, 0, 0))],
        out_shape=[jax.ShapeDtypeStruct((b, n, 2 * width), BF16),
                   jax.ShapeDtypeStruct((b, 1, width), F32)],
        compiler_params=_params("parallel", "arbitrary"),
        name="fnet_chan_dft",
    )(u3, cs)


def _fnet_seq_body(gc, scale, c_ref, s_ref, z_ref, za_ref, w_ref, lo_ref, hi_ref, acc_c_ref, acc_s_ref):
    i = pl.program_id(1)
    k = pl.program_id(2)
    half = N_GROUPS * gc

    @pl.when(k == 0)
    def _():
        acc_c_ref[...] = jnp.zeros_like(acc_c_ref)
        acc_s_ref[...] = jnp.zeros_like(acc_s_ref)

    acc_c_ref[...] += _dot(c_ref[...], z_ref[:, :half])
    acc_s_ref[...] += _dot(s_ref[...], z_ref[:, half:])

    @pl.when(k == pl.num_programs(2) - 1)
    def _():
        tm = acc_c_ref.shape[0]
        first_row = (lax.broadcasted_iota(jnp.int32, (tm, 1), 0) + i * tm) == 0
        for g in range(N_GROUPS):
            sl = slice(g * gc, (g + 1) * gc)
            a, bq = acc_c_ref[:, sl], acc_s_ref[:, sl]
            lo_ref[:, sl] = _dot(((a - bq) * scale).astype(BF16), w_ref[g]).astype(BF16)
            hi = _dot(((a + bq) * scale).astype(BF16), w_ref[g])
            mid = jnp.broadcast_to(za_ref[:, sl], (8, gc)) * scale
            mid = _dot(mid.astype(BF16), w_ref[g])[0:1, :]
            hi_ref[:, sl] = jnp.where(first_row, mid, hi).astype(BF16)


def _fnet_seq_call(top_c, top_s, z, z_alt, w):
    b, n, two_w = z.shape
    width = two_w // 2
    gc = width // N_GROUPS
    tm = _tile(n // 2, 1024)
    tk = _tile(n, 512)
    scale = 1.0 / math.sqrt(n * gc)
    out = pl.BlockSpec((None, tm, width), lambda bi, i, k: (bi, i, 0))
    return pl.pallas_call(
        functools.partial(_fnet_seq_body, gc, scale),
        grid=(b, (n // 2) // tm, n // tk),
        in_specs=[
            pl.BlockSpec((tm, tk), lambda bi, i, k: (i, k)),
            pl.BlockSpec((tm, tk), lambda bi, i, k: (i, k)),
            pl.BlockSpec((None, tk, two_w), lambda bi, i, k: (bi, k, 0)),
            pl.BlockSpec((None, 1, width), lambda bi, i, k: (bi, 0, 0)),
            pl.BlockSpec((N_GROUPS, gc, gc), lambda bi, i, k: (0, 0, 0)),
        ],
        out_specs=[out, out],
        out_shape=[jax.ShapeDtypeStruct((b, n // 2, width), BF16)] * 2,
        scratch_shapes=[pltpu.VMEM((tm, width), F32), pltpu.VMEM((tm, width), F32)],
        compiler_params=_params("parallel", "parallel", "arbitrary"),
        name="fnet_seq_dft",
    )(top_c, top_s, z, z_alt, w)


def _hy_pre_body(n, u0_ref, u1_ref, u2_ref, w0_ref, w1_ref, w2_ref, b0_ref, b1_ref, b2_ref,
                 x0_ref, vv_ref, vn_ref):
    row = lax.broadcasted_iota(jnp.int32, u0_ref.shape, 0)

    def conv(u_ref, w_ref, b_ref):
        u = u_ref[...].astype(F32)
        prev = jnp.where(row == 0, 0.0, pltpu.roll(u, 1, 0))
        nxt = jnp.where(row == n - 1, 0.0, pltpu.roll(u, n - 1, 0))
        return prev * w_ref[0:1, :] + u * w_ref[1:2, :] + nxt * w_ref[2:3, :] + b_ref[...]

    x0_ref[...] = conv(u0_ref, w0_ref, b0_ref).astype(BF16)
    vv = conv(u2_ref, w2_ref, b2_ref) * conv(u1_ref, w1_ref, b1_ref)
    vv_ref[...] = vv.astype(BF16)
    vn_ref[...] = jnp.sum(jnp.where(row % 2 == 0, vv, -vv), axis=0, keepdims=True)


def _hy_pre_call(u3, col0, width, conv_w, conv_b):
    b, n, _ = u3.shape
    tc = _tile(width, LANE)
    nct = width // tc

    def u_spec(s):
        return pl.BlockSpec((None, n, tc), lambda bi, c: (bi, 0, (col0 + s * width) // tc + c))

    def w_spec(s):
        return pl.BlockSpec((3, tc), lambda bi, c: (0, s * nct + c))

    def b_spec(s):
        return pl.BlockSpec((1, tc), lambda bi, c: (0, s * nct + c))

    out3 = pl.BlockSpec((None, n, tc), lambda bi, c: (bi, 0, c))
    return pl.pallas_call(
        functools.partial(_hy_pre_body, n),
        grid=(b, nct),
        in_specs=[u_spec(0), u_spec(1), u_spec(2), w_spec(0), w_spec(1), w_spec(2),
                  b_spec(0), b_spec(1), b_spec(2)],
        out_specs=[out3, out3, pl.BlockSpec((None, 1, tc), lambda bi, c: (bi, 0, c))],
        out_shape=[jax.ShapeDtypeStruct((b, n, width), BF16),
                   jax.ShapeDtypeStruct((b, n, width), BF16),
                   jax.ShapeDtypeStruct((b, 1, width), F32)],
        compiler_params=_params("parallel", "parallel"),
        name="hyena_short_conv",
    )(u3, u3, u3, conv_w, conv_w, conv_w, conv_b, conv_b, conv_b)


def _hy_filter_body(z_ref, w1_ref, b1_ref, w2_ref, b2_ref, fr_ref, wf_ref, wb_ref, dec_ref,
                    hs_ref, hd_ref, tn_ref):
    hp = lax.Precision.HIGHEST
    z = z_ref[...]
    fr = fr_ref[...]
    h = jnp.sin(fr * (jnp.dot(z, w1_ref[...], precision=hp, preferred_element_type=F32) + b1_ref[...]))
    h = jnp.sin(fr * (jnp.dot(h, w2_ref[...], precision=hp, preferred_element_type=F32) + b2_ref[...]))
    window = jnp.exp(-z[:, 0:1] * jnp.abs(dec_ref[...]))
    hf = jnp.dot(h, wf_ref[...], precision=hp, preferred_element_type=F32) * window
    hb = jnp.dot(h, wb_ref[...], precision=hp, preferred_element_type=F32) * window
    row = lax.broadcasted_iota(jnp.int32, hf.shape, 0)
    hb = jnp.where(row == 0, 0.0, hb)
    l1 = jnp.sum(jnp.abs(hf), axis=0, keepdims=True) + jnp.sum(jnp.abs(hb), axis=0, keepdims=True)
    hs = (hf + hb) / l1
    hd = (hb - hf) / l1
    hs_ref[...] = hs.astype(BF16)
    hd_ref[...] = hd.astype(BF16)
    tn_ref[...] = jnp.sum(jnp.where(row % 2 == 0, hs, -hs), axis=0, keepdims=True)


def _hy_filter_call(z, w1, b1, w2, b2, freq, w_out, decay):
    n, kz = z.shape
    hid = w1.shape[1]
    width = decay.shape[1]
    tc = _tile(width, 256)
    nct = width // tc
    full = lambda a: pl.BlockSpec(a.shape, lambda c: (0, 0))
    col = pl.BlockSpec((n, tc), lambda c: (0, c))
    vec = pl.BlockSpec((1, tc), lambda c: (0, c))
    return pl.pallas_call(
        _hy_filter_body,
        grid=(nct,),
        in_specs=[full(z), full(w1), full(b1), full(w2), full(b2), full(freq),
                  pl.BlockSpec((hid, tc), lambda c: (0, c)),
                  pl.BlockSpec((hid, tc), lambda c: (0, c + nct)),
                  vec],
        out_specs=[col, col, vec],
        out_shape=[jax.ShapeDtypeStruct((n, width), BF16),
                   jax.ShapeDtypeStruct((n, width), BF16),
                   jax.ShapeDtypeStruct((1, width), F32)],
        compiler_params=_params("parallel"),
        name="hyena_filter",
    )(z, w1, b1, w2, b2, freq, w_out, w_out, decay)


def _hy_spec_body(c_ref, s_ref, hs_ref, hd_ref, tre_ref, tim_ref):
    @pl.when(pl.program_id(1) == 0)
    def _():
        tre_ref[...] = jnp.zeros_like(tre_ref)
        tim_ref[...] = jnp.zeros_like(tim_ref)

    tre_ref[...] += _dot(c_ref[...], hs_ref[...])
    tim_ref[...] += _dot(s_ref[...], hd_ref[...])


def _hy_spec_call(cq, sq, hs, hd):
    n, width = hs.shape
    tm = _tile(n, 512)
    tk = _tile(n, 512)
    mat = pl.BlockSpec((tm, tk), lambda i, k: (i, k))
    rhs = pl.BlockSpec((tk, width), lambda i, k: (k, 0))
    out = pl.BlockSpec((tm, width), lambda i, k: (i, 0))
    return pl.pallas_call(
        _hy_spec_body,
        grid=(n // tm, n // tk),
        in_specs=[mat, mat, rhs, rhs],
        out_specs=[out, out],
        out_shape=[jax.ShapeDtypeStruct((n, width), F32)] * 2,
        compiler_params=_params("parallel", "arbitrary"),
        name="hyena_filter_spectrum",
    )(cq, sq, hs, hd)


def _hy_fwd_body(n, width, c_ref, s_ref, v_ref, tre_ref, tim_ref, p_ref, ar_ref, ai_ref):
    k = pl.program_id(2)

    @pl.when(k == 0)
    def _():
        ar_ref[...] = jnp.zeros_like(ar_ref)
        ai_ref[...] = jnp.zeros_like(ai_ref)

    v = v_ref[...]
    ar_ref[...] += _dot(c_ref[...], v)
    ai_ref[...] += _dot(s_ref[...], v)

    @pl.when(k == pl.num_programs(2) - 1)
    def _():
        tm = ar_ref.shape[0]
        freq = lax.broadcasted_iota(jnp.int32, (tm, 1), 0) + pl.program_id(1) * tm
        wk = jnp.where(freq == 0, 0.5 / n, 1.0 / n)
        ar, ai, tre, tim = ar_ref[...], ai_ref[...], tre_ref[...], tim_ref[...]
        p_ref[:, :width] = ((ar * tre + ai * tim) * wk).astype(BF16)
        p_ref[:, width:] = ((ar * tim - ai * tre) * wk).astype(BF16)


def _hy_fwd_call(cq, sq, vv, tre, tim):
    b, n, width = vv.shape
    tm = _tile(n, 1024)
    tk = _tile(n, 512)
    mat = pl.BlockSpec((tm, tk), lambda bi, i, k: (i, k))
    spec = pl.BlockSpec((tm, width), lambda bi, i, k: (i, 0))
    return pl.pallas_call(
        functools.partial(_hy_fwd_body, n, width),
        grid=(b, n // tm, n // tk),
        in_specs=[mat, mat, pl.BlockSpec((None, tk, width), lambda bi, i, k: (bi, k, 0)), spec, spec],
        out_specs=pl.BlockSpec((None, tm, 2 * width), lambda bi, i, k: (bi, i, 0)),
        out_shape=jax.ShapeDtypeStruct((b, n, 2 * width), BF16),
        scratch_shapes=[pltpu.VMEM((tm, width), F32), pltpu.VMEM((tm, width), F32)],
        compiler_params=_params("parallel", "parallel", "arbitrary"),
        name="hyena_fwd_dft",
    )(cq, sq, vv, tre, tim)


def _hy_inv_body(n, width, c_ref, s_ref, p_ref, x0_ref, vv_ref, vn_ref, tn_ref, bias_ref, o_ref, acc_ref):
    k = pl.program_id(2)

    @pl.when(k == 0)
    def _():
        acc_ref[...] = jnp.zeros_like(acc_ref)

    acc_ref[...] += _dot(c_ref[...], p_ref[:, :width]) - _dot(s_ref[...], p_ref[:, width:])

    @pl.when(k == pl.num_programs(2) - 1)
    def _():
        tm = acc_ref.shape[0]
        pos = lax.broadcasted_iota(jnp.int32, (tm, 1), 0) + pl.program_id(1) * tm
        sign = jnp.where(pos % 2 == 0, 1.0, -1.0)
        nyq = (vn_ref[...] * tn_ref[...]) * (0.5 / n)
        conv = acc_ref[...] + sign * nyq
        o_ref[...] = (x0_ref[...].astype(F32) * (conv + vv_ref[...].astype(F32) * bias_ref[...])).astype(BF16)


def _hy_inv_call(cq, sq, p, x0, vv, vn, tn, bias):
    b, n, width = vv.shape
    tm = _tile(n, 1024)
    tk = _tile(n, 512)
    mat = pl.BlockSpec((tm, tk), lambda bi, i, k: (i, k))
    tok = pl.BlockSpec((None, tm, width), lambda bi, i, k: (bi, i, 0))
    vec = pl.BlockSpec((1, width), lambda bi, i, k: (0, 0))
    return pl.pallas_call(
        functools.partial(_hy_inv_body, n, width),
        grid=(b, n // tm, n // tk),
        in_specs=[mat, mat, pl.BlockSpec((None, tk, 2 * width), lambda bi, i, k: (bi, k, 0)),
                  tok, tok, pl.BlockSpec((None, 1, width), lambda bi, i, k: (bi, 0, 0)), vec, vec],
        out_specs=tok,
        out_shape=jax.ShapeDtypeStruct((b, n, width), BF16),
        scratch_shapes=[pltpu.VMEM((tm, width), F32)],
        compiler_params=_params("parallel", "parallel", "arbitrary"),
        name="hyena_inv_dft",
    )(cq, sq, p, x0, vv, vn, tn, bias)


def _ret_body(dk, ch, nc, lgf_ref, lgb_ref, q_ref, k_ref, v_ref, g_ref, gn_ref, s0f_ref, s0b_ref,
              y_ref, sfo_ref, sbo_ref, o_ref, sf_ref, sb_ref):
    h = pl.program_id(1)
    sf_ref[...] = s0f_ref[...]
    sb_ref[...] = s0b_ref[...]

    ii = lax.broadcasted_iota(jnp.int32, (ch, ch), 0)
    jj = lax.broadcasted_iota(jnp.int32, (ch, ch), 1)
    diff = (ii - jj).astype(F32)
    idx = lax.broadcasted_iota(jnp.int32, (ch, 1), 0).astype(F32)
    k_scale = dk ** -0.5

    def decays(lg, lag, q_pow, k_pow):
        inner = jnp.where(lag >= 0, jnp.exp(lg * jnp.maximum(lag, 0.0)), 0.0)
        return inner, jnp.exp(lg * q_pow), jnp.exp(lg * k_pow), jnp.exp(lg * jnp.full((1, 1), float(ch), F32))

    dec_f = decays(lgf_ref[h], diff, idx + 1.0, (ch - 1.0) - idx)
    dec_b = decays(lgb_ref[h], -diff, float(ch) - idx, idx)

    def rows(c):
        return pl.ds(pl.multiple_of(c * ch, ch), ch)

    def chunk(c, dec, s_ref):
        inner_decay, q_decay, k_decay, chunk_decay = dec
        q = q_ref[rows(c), :]
        k = k_ref[rows(c), :].astype(F32) * k_scale
        v = v_ref[rows(c), :]
        scores = lax.dot_general(q, k.astype(BF16), (((1,), (1,)), ((), ())),
                                 preferred_element_type=F32) * inner_decay
        s = s_ref[...]
        out = _dot(scores.astype(BF16), v) + _dot(q, s.astype(BF16)) * q_decay
        kd = (k * k_decay).T.astype(BF16)
        s_ref[...] = s * chunk_decay + _dot(kd, v)
        return out

    def finish(c, o):
        mu = jnp.mean(o, axis=-1, keepdims=True)
        var = jnp.mean(jnp.square(o - mu), axis=-1, keepdims=True)
        on = (o - mu) * lax.rsqrt(var + EPS) * gn_ref[...]
        y_ref[rows(c), :] = (_silu(g_ref[rows(c), :].astype(F32)) * on).astype(BF16)

    def first_half(c, carry):
        cb = nc - 1 - c
        o_ref[rows(c), :] = chunk(c, dec_f, sf_ref)
        o_ref[rows(cb), :] = chunk(cb, dec_b, sb_ref)
        return carry

    def second_half(c, carry):
        cb = nc - 1 - c
        finish(c, o_ref[rows(c), :] + chunk(c, dec_f, sf_ref))
        finish(cb, o_ref[rows(cb), :] + chunk(cb, dec_b, sb_ref))
        return carry

    lax.fori_loop(0, nc // 2, first_half, 0)
    lax.fori_loop(nc // 2, nc, second_half, 0)
    sfo_ref[...] = sf_ref[...]
    sbo_ref[...] = sb_ref[...]


def _ret_call(u3, width, lgf, lgb, gn, s0f, s0b):
    b, n, _ = u3.shape
    dk = width // N_GROUPS
    ch = min(RET_CHUNK, n // 2)
    nc = n // ch
    assert nc % 2 == 0 and nc * ch == n

    def tok(part):
        return pl.BlockSpec((None, n, dk), lambda bi, h: (bi, 0, part * N_GROUPS + h))

    state = pl.BlockSpec((None, None, dk, dk), lambda bi, h: (bi, h, 0, 0))
    smem = pl.BlockSpec(memory_space=pltpu.SMEM)
    return pl.pallas_call(
        functools.partial(_ret_body, dk, ch, nc),
        grid=(b, N_GROUPS),
        in_specs=[smem, smem, tok(0), tok(1), tok(2), tok(3),
                  pl.BlockSpec((1, dk), lambda bi, h: (0, h)), state, state],
        out_specs=[pl.BlockSpec((None, n, dk), lambda bi, h: (bi, 0, h)), state, state],
        out_shape=[jax.ShapeDtypeStruct((b, n, width), BF16),
                   jax.ShapeDtypeStruct((b, N_GROUPS, dk, dk), F32),
                   jax.ShapeDtypeStruct((b, N_GROUPS, dk, dk), F32)],
        scratch_shapes=[pltpu.VMEM((n, dk), F32), pltpu.VMEM((dk, dk), F32), pltpu.VMEM((dk, dk), F32)],
        compiler_params=_params("parallel", "parallel"),
        name="retention",
    )(lgf, lgb, u3, u3, u3, u3, gn, s0f, s0b)


def _pool_body(n, x_ref, w_ref, sc_ref, y_ref):
    g = pl.program_id(1)
    gc = x_ref.shape[1]
    n_pad = n + 2 * POOL_PAD
    pos = lax.broadcasted_iota(jnp.int32, (n, 1), 0)

    for gi, win in enumerate(POOL_WINDOWS):
        @pl.when(g == gi)
        def _(win=win):
            x = x_ref[...].astype(F32)
            zeros = jnp.zeros((POOL_PAD, gc), F32)
            s = jnp.concatenate([zeros, x, zeros], axis=0)
            span = 1
            while span < win:
                s = s + pltpu.roll(s, span, 0)
                span *= 2
            total = pltpu.roll(s, n_pad - (POOL_PAD + win // 2 - 1), 0)[:n]
            lo = jnp.clip(pos - win // 2, 0, n - 1)
            hi = jnp.clip(pos - win // 2 + win - 1, 0, n - 1)
            count = (hi - lo + 1).astype(F32)
            p = total / count - x
            y = _dot(p.astype(BF16), w_ref[...]) * sc_ref[...]
            y_ref[...] = y.astype(BF16)


def _pool_call(u3, col0, width, w, scale):
    b, n, _ = u3.shape
    gc = width // N_GROUPS
    assert max(POOL_WINDOWS) // 2 <= POOL_PAD
    return pl.pallas_call(
        functools.partial(_pool_body, n),
        grid=(b, N_GROUPS),
        in_specs=[pl.BlockSpec((None, n, gc), lambda bi, g: (bi, 0, col0 // gc + g)),
                  pl.BlockSpec((None, gc, gc), lambda bi, g: (g, 0, 0)),
                  pl.BlockSpec((1, gc), lambda bi, g: (0, g))],
        out_specs=pl.BlockSpec((None, n, gc), lambda bi, g: (bi, 0, g)),
        out_shape=jax.ShapeDtypeStruct((b, n, width), BF16),
        compiler_params=_params("parallel", "parallel"),
        name="pool_mix",
    )(u3, w, scale)


def _dft_tables(rows, cols, period):
    assert period % 8 == 0
    k = jnp.arange(rows, dtype=jnp.int32)
    s = jnp.arange(cols, dtype=jnp.int32)
    m = (k[:, None] * s[None, :]) % period
    quarter = period // 4
    quad = m // quarter
    r = m % quarter
    swap = r > quarter // 2
    x = jnp.where(swap, quarter - r, r).astype(F32) * (2.0 * math.pi / period)
    x2 = x * x
    sin_x = x * (1.0 + x2 * (-1.0 / 6 + x2 * (1.0 / 120 + x2 * (-1.0 / 5040 + x2 * (1.0 / 362880)))))
    cos_x = 1.0 + x2 * (-0.5 + x2 * (1.0 / 24 + x2 * (-1.0 / 720 + x2 * (1.0 / 40320 - x2 * (1.0 / 3628800)))))
    c0 = jnp.where(swap, sin_x, cos_x)
    s0 = jnp.where(swap, cos_x, sin_x)
    cos_t = jnp.where(quad == 0, c0, jnp.where(quad == 1, -s0, jnp.where(quad == 2, -c0, s0)))
    sin_t = jnp.where(quad == 0, s0, jnp.where(quad == 1, c0, jnp.where(quad == 2, -s0, -c0)))
    return cos_t.astype(BF16), sin_t.astype(BF16)


def _grid_pos_embed(n_tokens, d):
    rows = n_tokens // GRID_W
    rr, cc = jnp.meshgrid(jnp.arange(rows, dtype=F32), jnp.arange(GRID_W, dtype=F32), indexing='ij')
    rr = rr.reshape(-1)[:, None]
    cc = cc.reshape(-1)[:, None]
    quarter = d // 4
    omega = 1.0 / (POS_BASE ** (jnp.arange(quarter, dtype=F32) / quarter))
    ar, ac = rr * omega, cc * omega
    return jnp.concatenate([jnp.sin(ar), jnp.cos(ar), jnp.sin(ac), jnp.cos(ac)], axis=-1)


def _hyena_pos_features(n):
    t_idx = jnp.arange(n, dtype=F32)
    t = jnp.linspace(0.0, 1.0, n, dtype=F32)
    bands = jnp.linspace(1e-4, HY_BANDS - 1, HY_BANDS, dtype=F32)
    ang = (2.0 * math.pi / n) * t_idx[:, None] * bands[None, :]
    z = jnp.concatenate([t[:, None], jnp.cos(ang), jnp.sin(ang)], axis=-1)
    return jnp.pad(z, ((0, 0), (0, LANE - z.shape[1])))


def _pad2(a, rows, cols):
    return jnp.pad(a, ((0, rows - a.shape[0]), (0, cols - a.shape[1])))


def kernel(x_prompt, x_sample, state_ret_fwd, state_ret_bwd, c, c_ctx, norm_g, mod_w, mod_b, ffn_a_in, ffn_a_out, ffn_b_in, ffn_b_out, ev_in_w, ev_out_w, fnet_w, hy_conv_w, hy_conv_b, hy_w1, hy_b1, hy_w2, hy_b2, hy_w_out, hy_freq, hy_decay, hy_bias, od_in_w, od_out_w, ret_log_decay_fwd, ret_log_decay_bwd, ret_gn, pool_w, pool_scale, final_norm):
    depth, d = norm_g.shape[0], norm_g.shape[2]
    half = d // 2
    dec_b = x_sample.shape[0]
    assert dec_b + 1 <= MOD_ROWS

    cond = jnp.concatenate([c, c_ctx[None, :]], axis=0)
    cond = jnp.pad(cond, ((0, MOD_ROWS - cond.shape[0]), (0, 0)))
    mod = _mod_call(cond, mod_w, mod_b).reshape(depth, MOD_ROWS, N_MOD, 1, d)

    bf = lambda a: a.astype(BF16)
    ffn_a_in, ffn_a_out, ffn_b_in, ffn_b_out = bf(ffn_a_in), bf(ffn_a_out), bf(ffn_b_in), bf(ffn_b_out)
    ev_in_w, ev_out_w, od_in_w, od_out_w = bf(ev_in_w), bf(ev_out_w), bf(od_in_w), bf(od_out_w)
    fnet_w, pool_w = bf(fnet_w), bf(pool_w)

    hid = hy_w1.shape[2]
    n_even = hy_w1.shape[0]
    hy_w1p = jnp.stack([_pad2(hy_w1[e], LANE, LANE) for e in range(n_even)])
    hy_w2p = jnp.stack([_pad2(hy_w2[e], LANE, LANE) for e in range(n_even)])
    hy_woutp = jnp.pad(hy_w_out, ((0, 0), (0, LANE - hid), (0, 0)))
    padv = lambda a: jnp.pad(a, ((0, 0), (0, LANE - hid)))[:, None, :]
    hy_b1p, hy_b2p, hy_freqp = padv(hy_b1), padv(hy_b2), padv(hy_freq)

    gc = half // N_GROUPS
    cs_chan = jnp.concatenate(_dft_tables(gc, gc, gc), axis=1)

    def trunk(x3, mod_rows, s0_f, s0_b, pos=None):
        b, n, _ = x3.shape
        t = b * n
        rows_per_mod = n if mod_rows.stop - mod_rows.start > 1 else t
        x = x3.reshape(t, d)
        cq, sq = _dft_tables(n, n, 2 * n)
        top_c, top_s = _dft_tables(n // 2, n, n)
        z_pos = _hyena_pos_features(n)
        s_f_out, s_b_out = [], []
        for l in range(depth):
            m = [mod[l, mod_rows, i] for i in range(N_MOD)]
            x = _ffn_call(x, norm_g[l, 0][None], m[0], m[1], m[2], ffn_a_in, ffn_a_out, l, rows_per_mod,
                          pos=pos if l == 0 else None)
            if l % 2 == 0:
                e = l // 2
                u3 = _inproj_call(x, norm_g[l, 1][None], m[3], m[4], ev_in_w, e, rows_per_mod).reshape(b, n, -1)
                zc, z_alt = _fnet_chan_call(u3, cs_chan, half)
                lo, hi = _fnet_seq_call(top_c, top_s, zc, z_alt, fnet_w[e])
                ya = jnp.concatenate([lo, hi[:, :1], hi[:, :0:-1]], axis=1)
                x0, vv, vn = _hy_pre_call(u3, half, half, hy_conv_w[e], hy_conv_b[e][None])
                hs, hd, tn = _hy_filter_call(z_pos, hy_w1p[e], hy_b1p[e], hy_w2p[e], hy_b2p[e], hy_freqp[e],
                                             hy_woutp[e], hy_decay[e][None])
                tre, tim = _hy_spec_call(cq, sq, hs, hd)
                p = _hy_fwd_call(cq, sq, vv, tre, tim)
                yb = _hy_inv_call(cq, sq, p, x0, vv, vn, tn, hy_bias[e][None])
                w_out, w_idx = ev_out_w, e
            else:
                o = l // 2
                u3 = _inproj_call(x, norm_g[l, 1][None], m[3], m[4], od_in_w, o, rows_per_mod).reshape(b, n, -1)
                ya, s_f, s_b = _ret_call(u3, half, ret_log_decay_fwd[o], ret_log_decay_bwd[o], ret_gn[o][None],
                                         s0_f[:, o], s0_b[:, o])
                yb = _pool_call(u3, 4 * half, half, pool_w[o], pool_scale[o][None])
                s_f_out.append(s_f)
                s_b_out.append(s_b)
                w_out, w_idx = od_out_w, o
            x = _outproj_call(x, ya.reshape(t, half), yb.reshape(t, half), w_out, w_idx, m[5], rows_per_mod)
            x = _ffn_call(x, norm_g[l, 2][None], m[6], m[7], m[8], ffn_b_in, ffn_b_out, l, rows_per_mod,
                          final_g=final_norm[None] if l == depth - 1 else None)
        return x.reshape(b, n, d), jnp.stack(s_f_out, axis=1), jnp.stack(s_b_out, axis=1)

    zero_state = jnp.zeros((x_prompt.shape[0],) + state_ret_fwd.shape[1:], F32)
    y_prompt, st_f, st_b = trunk(x_prompt, slice(dec_b, dec_b + 1), zero_state, zero_state)

    y_sample, _, _ = trunk(x_sample, slice(0, dec_b), state_ret_fwd, state_ret_bwd,
                           pos=_grid_pos_embed(x_sample.shape[1], d))
    return (y_prompt, y_sample, st_f, st_b)
```

```python
import functools
import math

import jax
import jax.numpy as jnp
from jax import lax
from jax.experimental import pallas as pl
from jax.experimental.pallas import tpu as pltpu

F32 = jnp.float32
BF16 = jnp.bfloat16

EPS = 1e-6
N_MOD = 9
GRID_W = 64
POS_BASE = 10000.0
N_GROUPS = 4
HY_BANDS = 16
RET_CHUNK = 256
POOL_WINDOWS = (2, 4, 8, 16)
POOL_PAD = 8
LANE = 128
MOD_ROWS = 16
VMEM_LIMIT = 56 * 1024 * 1024
NORM_ROWS = 32
NORM_UNROLL = 4
FFN_TM = 1024
FFN_TF = 512


def _params(*sem):
    return pltpu.CompilerParams(dimension_semantics=sem, vmem_limit_bytes=VMEM_LIMIT)


def _tile(n, pref):
    if n <= pref:
        return n
    t = pref
    while n % t:
        t -= LANE
    assert t > 0, (n, pref)
    return t


def _silu(x):
    return x * (1.0 / (1.0 + jnp.exp(-x)))


def _norm_mod_into(h_ref, x_ref, g_ref, shift_ref, scale_ref):
    gs = g_ref[...] * (1.0 + scale_ref[...])
    shift = shift_ref[...]

    def body(r, carry):
        rows = pl.ds(pl.multiple_of(r * NORM_ROWS, NORM_ROWS), NORM_ROWS)
        x = x_ref[rows, :]
        ms = jnp.mean(x * x, axis=-1, keepdims=True)
        h_ref[rows, :] = (x * lax.rsqrt(ms + EPS) * gs + shift).astype(h_ref.dtype)
        return carry

    lax.fori_loop(0, x_ref.shape[0] // NORM_ROWS, body, 0, unroll=NORM_UNROLL)


def _dot(a, b):
    return jnp.dot(a, b, preferred_element_type=F32)


def _mod_body(c_ref, w_ref, b_ref, o_ref):
    s = _silu(c_ref[...]).astype(BF16)
    o_ref[...] = _dot(s, w_ref[...].astype(BF16)) + b_ref[...]


def _mod_call(cond, mod_w, mod_b):
    depth, d, n_out = mod_w.shape
    tn = _tile(n_out, 1024)
    return pl.pallas_call(
        _mod_body,
        grid=(depth, n_out // tn),
        in_specs=[
            pl.BlockSpec((MOD_ROWS, d), lambda l, j: (0, 0)),
            pl.BlockSpec((None, d, tn), lambda l, j: (l, 0, j)),
            pl.BlockSpec((None, 1, tn), lambda l, j: (l, 0, j)),
        ],
        out_specs=pl.BlockSpec((None, MOD_ROWS, tn), lambda l, j: (l, 0, j)),
        out_shape=jax.ShapeDtypeStruct((depth, MOD_ROWS, n_out), F32),
        compiler_params=_params("parallel", "parallel"),
        name="mod_proj",
    )(cond, mod_w, mod_b.reshape(depth, 1, n_out))


def _ffn_body(final, with_pos, x_ref, g_ref, sh_ref, sc_ref, gt_ref, wg_ref, wu_ref, wo_ref, *rest):
    rest = list(rest)
    pos_ref = rest.pop(0) if with_pos else None
    fg_ref = rest.pop(0) if final else None
    o_ref, h_ref = rest
    j = pl.program_id(1)

    @pl.when(j == 0)
    def _():
        if with_pos:
            o_ref[...] = x_ref[...] + pos_ref[...]
            _norm_mod_into(h_ref, o_ref, g_ref, sh_ref, sc_ref)
        else:
            _norm_mod_into(h_ref, x_ref, g_ref, sh_ref, sc_ref)
            o_ref[...] = x_ref[...]

    h = h_ref[...]
    gate = _dot(h, wg_ref[...])
    up = _dot(h, wu_ref[...])
    a = (_silu(gate) * up).astype(BF16)
    o_ref[...] += (0.5 * gt_ref[...]) * _dot(a, wo_ref[...])

    if final:
        @pl.when(j == pl.num_programs(1) - 1)
        def _():
            _rmsnorm_rows(o_ref, fg_ref)


def _rmsnorm_rows(o_ref, g_ref):
    g = g_ref[...]

    def body(r, carry):
        rows = pl.ds(pl.multiple_of(r * NORM_ROWS, NORM_ROWS), NORM_ROWS)
        y = o_ref[rows, :]
        ms = jnp.mean(y * y, axis=-1, keepdims=True)
        o_ref[rows, :] = y * lax.rsqrt(ms + EPS) * g
        return carry

    lax.fori_loop(0, o_ref.shape[0] // NORM_ROWS, body, 0, unroll=NORM_UNROLL)


def _ffn_call(x, g, shift, scale, gate, w_in, w_out, layer, rows_per_mod, final_g=None, pos=None):
    t, d = x.shape
    d_ff = w_out.shape[1]
    tm = _tile(rows_per_mod, FFN_TM // 2 if pos is not None else FFN_TM)
    tf = _tile(d_ff, FFN_TF)
    nf = d_ff // tf
    final = final_g is not None
    with_pos = pos is not None
    tiles_per_mod = rows_per_mod // tm
    mod_spec = pl.BlockSpec((None, 1, d), lambda i, j: (i // tiles_per_mod, 0, 0))
    vec_spec = pl.BlockSpec((1, d), lambda i, j: (0, 0))
    in_specs = [
        pl.BlockSpec((tm, d), lambda i, j: (i, 0)),
        vec_spec, mod_spec, mod_spec, mod_spec,
        pl.BlockSpec((None, d, tf), lambda i, j: (layer, 0, j)),
        pl.BlockSpec((None, d, tf), lambda i, j: (layer, 0, j + nf)),
        pl.BlockSpec((None, tf, d), lambda i, j: (layer, j, 0)),
    ]
    args = [x, g, shift, scale, gate, w_in, w_in, w_out]
    if with_pos:
        in_specs.append(pl.BlockSpec((tm, d), lambda i, j: (i % tiles_per_mod, 0)))
        args.append(pos)
    if final:
        in_specs.append(vec_spec)
        args.append(final_g)
    return pl.pallas_call(
        functools.partial(_ffn_body, final, with_pos),
        grid=(t // tm, nf),
        in_specs=in_specs,
        out_specs=pl.BlockSpec((tm, d), lambda i, j: (i, 0)),
        out_shape=jax.ShapeDtypeStruct((t, d), F32),
        scratch_shapes=[pltpu.VMEM((tm, d), BF16)],
        compiler_params=_params("parallel", "arbitrary"),
        name="ffn",
    )(*args)


def _inproj_body(x_ref, g_ref, sh_ref, sc_ref, w_ref, o_ref, h_ref):
    @pl.when(pl.program_id(1) == 0)
    def _():
        _norm_mod_into(h_ref, x_ref, g_ref, sh_ref, sc_ref)

    o_ref[...] = _dot(h_ref[...], w_ref[...]).astype(BF16)


def _inproj_call(x, g, shift, scale, w, layer, rows_per_mod):
    t, d = x.shape
    n_out = w.shape[2]
    tm = _tile(rows_per_mod, 1024)
    tn = _tile(n_out, 1024)
    mod_spec = pl.BlockSpec((None, 1, d), lambda i, j: ((i * tm) // rows_per_mod, 0, 0))
    return pl.pallas_call(
        _inproj_body,
        grid=(t // tm, n_out // tn),
        in_specs=[
            pl.BlockSpec((tm, d), lambda i, j: (i, 0)),
            pl.BlockSpec((1, d), lambda i, j: (0, 0)),
            mod_spec, mod_spec,
            pl.BlockSpec((None, d, tn), lambda i, j: (layer, 0, j)),
        ],
        out_specs=pl.BlockSpec((tm, tn), lambda i, j: (i, j)),
        out_shape=jax.ShapeDtypeStruct((t, n_out), BF16),
        scratch_shapes=[pltpu.VMEM((tm, d), BF16)],
        compiler_params=_params("parallel", "arbitrary"),
        name="mix_in_proj",
    )(x, g, shift, scale, w)


def _outproj_body(x_ref, ya_ref, yb_ref, wa_ref, wb_ref, gt_ref, o_ref):
    y = _dot(ya_ref[...], wa_ref[...]) + _dot(yb_ref[...], wb_ref[...])
    o_ref[...] = x_ref[...] + gt_ref[...] * y


def _outproj_call(x, ya, yb, w, layer, gate, rows_per_mod):
    t, d = x.shape
    half = ya.shape[1]
    tm = _tile(rows_per_mod, 512)
    return pl.pallas_call(
        _outproj_body,
        grid=(t // tm,),
        in_specs=[
            pl.BlockSpec((tm, d), lambda i: (i, 0)),
            pl.BlockSpec((tm, half), lambda i: (i, 0)),
            pl.BlockSpec((tm, half), lambda i: (i, 0)),
            pl.BlockSpec((None, half, d), lambda i: (layer, 0, 0)),
            pl.BlockSpec((None, half, d), lambda i: (layer, 1, 0)),
            pl.BlockSpec((None, 1, d), lambda i: ((i * tm) // rows_per_mod, 0, 0)),
        ],
        out_specs=pl.BlockSpec((tm, d), lambda i: (i, 0)),
        out_shape=jax.ShapeDtypeStruct((t, d), F32),
        compiler_params=_params("parallel"),
        name="mix_out_proj",
    )(x, ya, yb, w, w, gate)


def _fnet_chan_body(gc, x_ref, cs_ref, z_ref):
    half = N_GROUPS * gc
    for g in range(N_GROUPS):
        xg = x_ref[:, g * gc:(g + 1) * gc].astype(BF16)
        z = _dot(xg, cs_ref[...])
        z_ref[:, g * gc:(g + 1) * gc] = z[:, :gc].astype(BF16)
        z_ref[:, half + g * gc:half + (g + 1) * gc] = z[:, gc:].astype(BF16)


def _fnet_chan_call(u3, cs, width):
    b, n, _ = u3.shape
    gc = width // N_GROUPS
    tr = _tile(n, 512)
    return pl.pallas_call(
        functools.partial(_fnet_chan_body, gc),
        grid=(b, n // tr),
        in_specs=[
            pl.BlockSpec((None, tr, width), lambda bi, i: (bi, i, 0)),
            pl.BlockSpec((gc, 2 * gc), lambda bi, i: (0, 0)),
        ],
        out_specs=pl.BlockSpec((None, tr, 2 * width), lambda bi, i: (bi, i, 0)),
        out_shape=jax.ShapeDtypeStruct((b, n, 2 * width), BF16),
        compiler_params=_params("parallel", "parallel"),
        name="fnet_chan_dft",
    )(u3, cs)


def _fnet_seq_body(gc, scale, ce_ref, co_ref, se_ref, so_ref, z_ref, w_ref, o_ref, e_ref, od_ref):
    k = pl.program_id(2)
    width = N_GROUPS * gc

    @pl.when(k == 0)
    def _():
        e_ref[...] = jnp.zeros_like(e_ref)
        od_ref[...] = jnp.zeros_like(od_ref)

    e_ref[...] += _dot(ce_ref[...], z_ref[:, 0 * width:1 * width]) - _dot(se_ref[...], z_ref[:, 1 * width:2 * width])
    od_ref[...] += _dot(co_ref[...], z_ref[:, 2 * width:3 * width]) - _dot(so_ref[...], z_ref[:, 3 * width:4 * width])

    @pl.when(k == pl.num_programs(2) - 1)
    def _():
        for g in range(N_GROUPS):
            sl = slice(g * gc, (g + 1) * gc)
            even, odd = e_ref[:, sl], od_ref[:, sl]
            o_ref[0, :, sl] = _dot(((even + odd) * scale).astype(BF16), w_ref[g]).astype(BF16)
            o_ref[1, :, sl] = _dot(((even - odd) * scale).astype(BF16), w_ref[g]).astype(BF16)


def _fnet_seq_call(tables, z, w):
    b, n, two_w = z.shape
    width = two_w // 2
    h = n // 2
    gc = width // N_GROUPS
    tm = _tile(h, 1024)
    tk = _tile(h, 512)
    scale = 1.0 / math.sqrt(n * gc)
    mat = pl.BlockSpec((tm, tk), lambda bi, i, k: (i, k))
    out = pl.pallas_call(
        functools.partial(_fnet_seq_body, gc, scale),
        grid=(b, h // tm, h // tk),
        in_specs=[mat, mat, mat, mat,
                  pl.BlockSpec((None, tk, 4 * width), lambda bi, i, k: (bi, k, 0)),
                  pl.BlockSpec((N_GROUPS, gc, gc), lambda bi, i, k: (0, 0, 0))],
        out_specs=pl.BlockSpec((None, 2, tm, width), lambda bi, i, k: (bi, 0, i, 0)),
        out_shape=jax.ShapeDtypeStruct((b, 2, h, width), BF16),
        scratch_shapes=[pltpu.VMEM((tm, width), F32), pltpu.VMEM((tm, width), F32)],
        compiler_params=_params("parallel", "parallel", "arbitrary"),
        name="fnet_seq_dft",
    )(*tables, z.reshape(b, h, 4 * width), w)
    return out.reshape(b, n, width)


def _alt_sums(x, row):
    phase = row % 4
    even = jnp.sum(jnp.where(phase == 0, x, jnp.where(phase == 2, -x, 0.0)), axis=0, keepdims=True)
    odd = jnp.sum(jnp.where(phase == 1, x, jnp.where(phase == 3, -x, 0.0)), axis=0, keepdims=True)
    return even, odd


def _hy_pre_body(n, u0_ref, u1_ref, u2_ref, w0_ref, w1_ref, w2_ref, b0_ref, b1_ref, b2_ref,
                 x0_ref, vv_ref, me_ref, mo_ref):
    row = lax.broadcasted_iota(jnp.int32, u0_ref.shape, 0)

    def conv(u_ref, w_ref, b_ref):
        u = u_ref[...].astype(F32)
        prev = jnp.where(row == 0, 0.0, pltpu.roll(u, 1, 0))
        nxt = jnp.where(row == n - 1, 0.0, pltpu.roll(u, n - 1, 0))
        return prev * w_ref[0:1, :] + u * w_ref[1:2, :] + nxt * w_ref[2:3, :] + b_ref[...]

    x0_ref[...] = conv(u0_ref, w0_ref, b0_ref).astype(BF16)
    vv = (conv(u2_ref, w2_ref, b2_ref) * conv(u1_ref, w1_ref, b1_ref)).astype(BF16)
    vv_ref[...] = vv
    me_ref[...], mo_ref[...] = _alt_sums(vv.astype(F32), row)


def _hy_pre_call(u3, col0, width, conv_w, conv_b):
    b, n, _ = u3.shape
    tc = _tile(width, LANE)
    nct = width // tc

    def u_spec(s):
        return pl.BlockSpec((None, n, tc), lambda bi, c: (bi, 0, (col0 + s * width) // tc + c))

    def w_spec(s):
        return pl.BlockSpec((3, tc), lambda bi, c: (0, s * nct + c))

    def b_spec(s):
        return pl.BlockSpec((1, tc), lambda bi, c: (0, s * nct + c))

    out3 = pl.BlockSpec((None, n, tc), lambda bi, c: (bi, 0, c))
    vec = pl.BlockSpec((None, 1, tc), lambda bi, c: (bi, 0, c))
    return pl.pallas_call(
        functools.partial(_hy_pre_body, n),
        grid=(b, nct),
        in_specs=[u_spec(0), u_spec(1), u_spec(2), w_spec(0), w_spec(1), w_spec(2),
                  b_spec(0), b_spec(1), b_spec(2)],
        out_specs=[out3, out3, vec, vec],
        out_shape=[jax.ShapeDtypeStruct((b, n, width), BF16),
                   jax.ShapeDtypeStruct((b, n, width), BF16),
                   jax.ShapeDtypeStruct((b, 1, width), F32),
                   jax.ShapeDtypeStruct((b, 1, width), F32)],
        compiler_params=_params("parallel", "parallel"),
        name="hyena_short_conv",
    )(u3, u3, u3, conv_w, conv_w, conv_w, conv_b, conv_b, conv_b)


def _hy_filter_body(z_ref, w1_ref, b1_ref, w2_ref, b2_ref, fr_ref, wf_ref, wb_ref, dec_ref,
                    hs_ref, hd_ref, me_ref, mo_ref):
    hp = lax.Precision.HIGHEST
    z = z_ref[...]
    fr = fr_ref[...]
    h = jnp.sin(fr * (jnp.dot(z, w1_ref[...], precision=hp, preferred_element_type=F32) + b1_ref[...]))
    h = jnp.sin(fr * (jnp.dot(h, w2_ref[...], precision=hp, preferred_element_type=F32) + b2_ref[...]))
    window = jnp.exp(-z[:, 0:1] * jnp.abs(dec_ref[...]))
    hf = jnp.dot(h, wf_ref[...], precision=hp, preferred_element_type=F32) * window
    hb = jnp.dot(h, wb_ref[...], precision=hp, preferred_element_type=F32) * window
    row = lax.broadcasted_iota(jnp.int32, hf.shape, 0)
    hb = jnp.where(row == 0, 0.0, hb)
    l1 = jnp.sum(jnp.abs(hf), axis=0, keepdims=True) + jnp.sum(jnp.abs(hb), axis=0, keepdims=True)
    hs = ((hf + hb) / l1).astype(BF16)
    hd = ((hb - hf) / l1).astype(BF16)
    hs_ref[...] = hs
    hd_ref[...] = hd
    me_ref[...] = _alt_sums(hs.astype(F32), row)[0]
    mo_ref[...] = _alt_sums(hd.astype(F32), row)[1]


def _hy_filter_call(z, w1, b1, w2, b2, freq, w_out, decay):
    n, kz = z.shape
    hid = w1.shape[1]
    width = decay.shape[1]
    tc = _tile(width, 256)
    nct = width // tc
    full = lambda a: pl.BlockSpec(a.shape, lambda c: (0, 0))
    col = pl.BlockSpec((n, tc), lambda c: (0, c))
    vec = pl.BlockSpec((1, tc), lambda c: (0, c))
    return pl.pallas_call(
        _hy_filter_body,
        grid=(nct,),
        in_specs=[full(z), full(w1), full(b1), full(w2), full(b2), full(freq),
                  pl.BlockSpec((hid, tc), lambda c: (0, c)),
                  pl.BlockSpec((hid, tc), lambda c: (0, c + nct)),
                  vec],
        out_specs=[col, col, vec, vec],
        out_shape=[jax.ShapeDtypeStruct((n, width), BF16),
                   jax.ShapeDtypeStruct((n, width), BF16),
                   jax.ShapeDtypeStruct((1, width), F32),
                   jax.ShapeDtypeStruct((1, width), F32)],
        compiler_params=_params("parallel"),
        name="hyena_filter",
    )(z, w1, b1, w2, b2, freq, w_out, w_out, decay)


def _hy_zero(*refs):
    for r in refs:
        r[...] = jnp.zeros_like(r)


def _hy_spec_body(width, ce_ref, co_ref, se_ref, so_ref, hs_ref, hd_ref, t_ref, a_ref, b_ref, c_ref, d_ref):
    @pl.when(pl.program_id(1) == 0)
    def _():
        _hy_zero(a_ref, b_ref, c_ref, d_ref)

    a_ref[...] += _dot(ce_ref[...], hs_ref[:, :width])
    b_ref[...] += _dot(co_ref[...], hs_ref[:, width:])
    c_ref[...] += _dot(se_ref[...], hd_ref[:, :width])
    d_ref[...] += _dot(so_ref[...], hd_ref[:, width:])

    @pl.when(pl.program_id(1) == pl.num_programs(1) - 1)
    def _():
        a, b, c, d = a_ref[...], b_ref[...], c_ref[...], d_ref[...]
        t_ref[:, 0 * width:1 * width] = a + b
        t_ref[:, 1 * width:2 * width] = c + d
        t_ref[:, 2 * width:3 * width] = a - b
        t_ref[:, 3 * width:4 * width] = d - c


def _hy_spec_call(tables, hs2, hd2):
    h, two_w = hs2.shape
    width = two_w // 2
    tm = _tile(h, 512)
    tk = _tile(h, 512)
    mat = pl.BlockSpec((tm, tk), lambda i, k: (i, k))
    rhs = pl.BlockSpec((tk, two_w), lambda i, k: (k, 0))
    return pl.pallas_call(
        functools.partial(_hy_spec_body, width),
        grid=(h // tm, h // tk),
        in_specs=[mat, mat, mat, mat, rhs, rhs],
        out_specs=pl.BlockSpec((tm, 4 * width), lambda i, k: (i, 0)),
        out_shape=jax.ShapeDtypeStruct((h, 4 * width), F32),
        scratch_shapes=[pltpu.VMEM((tm, width), F32)] * 4,
        compiler_params=_params("parallel", "arbitrary"),
        name="hyena_filter_spectrum",
    )(*tables, hs2, hd2)


def _hy_fwd_body(n, width, ce_ref, co_ref, se_ref, so_ref, v_ref, t_ref, g_ref, a_ref, b_ref, c_ref, d_ref):
    i = pl.program_id(1)
    k = pl.program_id(2)

    @pl.when(k == 0)
    def _():
        _hy_zero(a_ref, b_ref, c_ref, d_ref)

    even, odd = v_ref[:, :width], v_ref[:, width:]
    a_ref[...] += _dot(ce_ref[...], even)
    b_ref[...] += _dot(co_ref[...], odd)
    c_ref[...] += _dot(se_ref[...], even)
    d_ref[...] += _dot(so_ref[...], odd)

    @pl.when(k == pl.num_programs(2) - 1)
    def _():
        tm = a_ref.shape[0]
        freq = lax.broadcasted_iota(jnp.int32, (tm, 1), 0) + i * tm
        wk = jnp.where(freq == 0, 0.5 / n, 1.0 / n)
        a, b, c, d = a_ref[...], b_ref[...], c_ref[...], d_ref[...]

        def product(v_cos, v_sin, t_re, t_im):
            return (v_cos * t_re + v_sin * t_im) * wk, (v_cos * t_im - v_sin * t_re) * wk

        re_lo, im_lo = product(a + b, c + d, t_ref[:, 0 * width:1 * width], t_ref[:, 1 * width:2 * width])
        re_hi, im_hi = product(a - b, d - c, t_ref[:, 2 * width:3 * width], t_ref[:, 3 * width:4 * width])
        g_ref[:, 0 * width:1 * width] = (re_lo + re_hi).astype(BF16)
        g_ref[:, 1 * width:2 * width] = (im_lo - im_hi).astype(BF16)
        g_ref[:, 2 * width:3 * width] = (re_lo - re_hi).astype(BF16)
        g_ref[:, 3 * width:4 * width] = (im_lo + im_hi).astype(BF16)


def _hy_fwd_call(tables, v2, t_spec):
    b, h, two_w = v2.shape
    width = two_w // 2
    tm = _tile(h, 512)
    tk = _tile(h, 512)
    mat = pl.BlockSpec((tm, tk), lambda bi, i, k: (i, k))
    return pl.pallas_call(
        functools.partial(_hy_fwd_body, 2 * h, width),
        grid=(b, h // tm, h // tk),
        in_specs=[mat, mat, mat, mat,
                  pl.BlockSpec((None, tk, two_w), lambda bi, i, k: (bi, k, 0)),
                  pl.BlockSpec((tm, 4 * width), lambda bi, i, k: (i, 0))],
        out_specs=pl.BlockSpec((None, tm, 4 * width), lambda bi, i, k: (bi, i, 0)),
        out_shape=jax.ShapeDtypeStruct((b, h, 4 * width), BF16),
        scratch_shapes=[pltpu.VMEM((tm, width), F32)] * 4,
        compiler_params=_params("parallel", "parallel", "arbitrary"),
        name="hyena_fwd_dft",
    )(*tables, v2, t_spec)


def _hy_inv_body(n, width, ce_ref, co_ref, se_ref, so_ref, g_ref, x0_ref, vv_ref, vme_ref, vmo_ref,
                 tme_ref, tmo_ref, bias_ref, o_ref, e_ref, od_ref):
    k = pl.program_id(2)

    @pl.when(k == 0)
    def _():
        _hy_zero(e_ref, od_ref)

    e_ref[...] += _dot(ce_ref[...], g_ref[:, 0 * width:1 * width]) - _dot(se_ref[...], g_ref[:, 1 * width:2 * width])
    od_ref[...] += _dot(co_ref[...], g_ref[:, 2 * width:3 * width]) - _dot(so_ref[...], g_ref[:, 3 * width:4 * width])

    @pl.when(k == pl.num_programs(2) - 1)
    def _():
        tm = e_ref.shape[0]
        sign = jnp.where(lax.broadcasted_iota(jnp.int32, (tm, 1), 0) % 2 == 0, 1.0, -1.0)
        v_e, v_o, t_e, t_o = vme_ref[...], vmo_ref[...], tme_ref[...], tmo_ref[...]
        mid_re = (v_e * t_e + v_o * t_o) * (1.0 / n)
        mid_im = (v_e * t_o - v_o * t_e) * (1.0 / n)
        conv_even = e_ref[...] + sign * mid_re
        conv_odd = od_ref[...] - sign * mid_im
        bias = bias_ref[...]

        def finish(cols, conv):
            x0 = x0_ref[:, cols].astype(F32)
            vv = vv_ref[:, cols].astype(F32)
            o_ref[:, cols] = (x0 * (conv + vv * bias)).astype(BF16)

        finish(slice(0, width), conv_even)
        finish(slice(width, 2 * width), conv_odd)


def _hy_inv_call(tables, g, x02, vv2, vm_e, vm_o, tm_e, tm_o, bias):
    b, h, two_w = vv2.shape
    width = two_w // 2
    tm = _tile(h, 512)
    tk = _tile(h, 512)
    assert tm % 2 == 0
    mat = pl.BlockSpec((tm, tk), lambda bi, i, k: (i, k))
    tok = pl.BlockSpec((None, tm, two_w), lambda bi, i, k: (bi, i, 0))
    bvec = pl.BlockSpec((None, 1, width), lambda bi, i, k: (bi, 0, 0))
    vec = pl.BlockSpec((1, width), lambda bi, i, k: (0, 0))
    return pl.pallas_call(
        functools.partial(_hy_inv_body, 2 * h, width),
        grid=(b, h // tm, h // tk),
        in_specs=[mat, mat, mat, mat,
                  pl.BlockSpec((None, tk, 4 * width), lambda bi, i, k: (bi, k, 0)),
                  tok, tok, bvec, bvec, vec, vec, vec],
        out_specs=tok,
        out_shape=jax.ShapeDtypeStruct((b, h, two_w), BF16),
        scratch_shapes=[pltpu.VMEM((tm, width), F32)] * 2,
        compiler_params=_params("parallel", "parallel", "arbitrary"),
        name="hyena_inv_dft",
    )(*tables, g, x02, vv2, vm_e, vm_o, tm_e, tm_o, bias)


def _hyena_mix(u3, col0, width, fwd_tables, inv_tables, z_pos, conv_w, conv_b, w1, b1, w2, b2, freq, w_out,
               decay, bias):
    b, n, _ = u3.shape
    fold = lambda a: a.reshape(a.shape[:-2] + (n // 2, 2 * width))
    x0, vv, vm_e, vm_o = _hy_pre_call(u3, col0, width, conv_w, conv_b)
    hs, hd, tm_e, tm_o = _hy_filter_call(z_pos, w1, b1, w2, b2, freq, w_out, decay)
    t_spec = _hy_spec_call(fwd_tables, fold(hs), fold(hd))
    g = _hy_fwd_call(fwd_tables, fold(vv), t_spec)
    y2 = _hy_inv_call(inv_tables, g, fold(x0), fold(vv), vm_e, vm_o, tm_e, tm_o, bias)
    return y2.reshape(b, n, width)


def _ret_body(dk, ch, nc, lgf_ref, lgb_ref, q_ref, k_ref, v_ref, g_ref, gn_ref, s0f_ref, s0b_ref,
              y_ref, sfo_ref, sbo_ref, o_ref, sf_ref, sb_ref):
    h = pl.program_id(1)
    sf_ref[...] = s0f_ref[...]
    sb_ref[...] = s0b_ref[...]

    ii = lax.broadcasted_iota(jnp.int32, (ch, ch), 0)
    jj = lax.broadcasted_iota(jnp.int32, (ch, ch), 1)
    diff = (ii - jj).astype(F32)
    idx = lax.broadcasted_iota(jnp.int32, (ch, 1), 0).astype(F32)
    k_scale = dk ** -0.5

    def decays(lg, lag, q_pow, k_pow):
        inner = jnp.where(lag >= 0, jnp.exp(lg * jnp.maximum(lag, 0.0)), 0.0)
        return inner, jnp.exp(lg * q_pow), jnp.exp(lg * k_pow), jnp.exp(lg * jnp.full((1, 1), float(ch), F32))

    dec_f = decays(lgf_ref[h], diff, idx + 1.0, (ch - 1.0) - idx)
    dec_b = decays(lgb_ref[h], -diff, float(ch) - idx, idx)

    def rows(c):
        return pl.ds(pl.multiple_of(c * ch, ch), ch)

    def chunk(c, dec, s_ref):
        inner_decay, q_decay, k_decay, chunk_decay = dec
        q = q_ref[rows(c), :]
        k = k_ref[rows(c), :].astype(F32) * k_scale
        v = v_ref[rows(c), :]
        scores = lax.dot_general(q, k.astype(BF16), (((1,), (1,)), ((), ())),
                                 preferred_element_type=F32) * inner_decay
        s = s_ref[...]
        out = _dot(scores.astype(BF16), v) + _dot(q, s.astype(BF16)) * q_decay
        kd = (k * k_decay).T.astype(BF16)
        s_ref[...] = s * chunk_decay + _dot(kd, v)
        return out

    def finish(c, o):
        mu = jnp.mean(o, axis=-1, keepdims=True)
        var = jnp.mean(jnp.square(o - mu), axis=-1, keepdims=True)
        on = (o - mu) * lax.rsqrt(var + EPS) * gn_ref[...]
        y_ref[rows(c), :] = (_silu(g_ref[rows(c), :].astype(F32)) * on).astype(BF16)

    def first_half(c, carry):
        cb = nc - 1 - c
        o_ref[rows(c), :] = chunk(c, dec_f, sf_ref)
        o_ref[rows(cb), :] = chunk(cb, dec_b, sb_ref)
        return carry

    def second_half(c, carry):
        cb = nc - 1 - c
        finish(c, o_ref[rows(c), :] + chunk(c, dec_f, sf_ref))
        finish(cb, o_ref[rows(cb), :] + chunk(cb, dec_b, sb_ref))
        return carry

    lax.fori_loop(0, nc // 2, first_half, 0)
    lax.fori_loop(nc // 2, nc, second_half, 0)
    sfo_ref[...] = sf_ref[...]
    sbo_ref[...] = sb_ref[...]


def _ret_call(u3, width, lgf, lgb, gn, s0f, s0b):
    b, n, _ = u3.shape
    dk = width // N_GROUPS
    ch = min(RET_CHUNK, n // 2)
    nc = n // ch
    assert nc % 2 == 0 and nc * ch == n

    def tok(part):
        return pl.BlockSpec((None, n, dk), lambda bi, h: (bi, 0, part * N_GROUPS + h))

    state = pl.BlockSpec((None, None, dk, dk), lambda bi, h: (bi, h, 0, 0))
    smem = pl.BlockSpec(memory_space=pltpu.SMEM)
    return pl.pallas_call(
        functools.partial(_ret_body, dk, ch, nc),
        grid=(b, N_GROUPS),
        in_specs=[smem, smem, tok(0), tok(1), tok(2), tok(3),
                  pl.BlockSpec((1, dk), lambda bi, h: (0, h)), state, state],
        out_specs=[pl.BlockSpec((None, n, dk), lambda bi, h: (bi, 0, h)), state, state],
        out_shape=[jax.ShapeDtypeStruct((b, n, width), BF16),
                   jax.ShapeDtypeStruct((b, N_GROUPS, dk, dk), F32),
                   jax.ShapeDtypeStruct((b, N_GROUPS, dk, dk), F32)],
        scratch_shapes=[pltpu.VMEM((n, dk), F32), pltpu.VMEM((dk, dk), F32), pltpu.VMEM((dk, dk), F32)],
        compiler_params=_params("parallel", "parallel"),
        name="retention",
    )(lgf, lgb, u3, u3, u3, u3, gn, s0f, s0b)


def _pool_body(n, x_ref, w_ref, sc_ref, y_ref):
    g = pl.program_id(1)
    gc = x_ref.shape[1]
    n_pad = n + 2 * POOL_PAD
    pos = lax.broadcasted_iota(jnp.int32, (n, 1), 0)

    for gi, win in enumerate(POOL_WINDOWS):
        @pl.when(g == gi)
        def _(win=win):
            x = x_ref[...].astype(F32)
            zeros = jnp.zeros((POOL_PAD, gc), F32)
            s = jnp.concatenate([zeros, x, zeros], axis=0)
            span = 1
            while span < win:
                s = s + pltpu.roll(s, span, 0)
                span *= 2
            total = pltpu.roll(s, n_pad - (POOL_PAD + win // 2 - 1), 0)[:n]
            lo = jnp.clip(pos - win // 2, 0, n - 1)
            hi = jnp.clip(pos - win // 2 + win - 1, 0, n - 1)
            count = (hi - lo + 1).astype(F32)
            p = total / count - x
            y = _dot(p.astype(BF16), w_ref[...]) * sc_ref[...]
            y_ref[...] = y.astype(BF16)


def _pool_call(u3, col0, width, w, scale):
    b, n, _ = u3.shape
    gc = width // N_GROUPS
    assert max(POOL_WINDOWS) // 2 <= POOL_PAD
    return pl.pallas_call(
        functools.partial(_pool_body, n),
        grid=(b, N_GROUPS),
        in_specs=[pl.BlockSpec((None, n, gc), lambda bi, g: (bi, 0, col0 // gc + g)),
                  pl.BlockSpec((None, gc, gc), lambda bi, g: (g, 0, 0)),
                  pl.BlockSpec((1, gc), lambda bi, g: (0, g))],
        out_specs=pl.BlockSpec((None, n, gc), lambda bi, g: (bi, 0, g)),
        out_shape=jax.ShapeDtypeStruct((b, n, width), BF16),
        compiler_params=_params("parallel", "parallel"),
        name="pool_mix",
    )(u3, w, scale)


def _dft_tables(row_vals, col_vals, period):
    assert period % 8 == 0
    k = row_vals.astype(jnp.int32)
    s = col_vals.astype(jnp.int32)
    m = (k[:, None] * s[None, :]) % period
    quarter = period // 4
    quad = m // quarter
    r = m % quarter
    swap = r > quarter // 2
    x = jnp.where(swap, quarter - r, r).astype(F32) * (2.0 * math.pi / period)
    x2 = x * x
    sin_x = x * (1.0 + x2 * (-1.0 / 6 + x2 * (1.0 / 120 + x2 * (-1.0 / 5040 + x2 * (1.0 / 362880)))))
    cos_x = 1.0 + x2 * (-0.5 + x2 * (1.0 / 24 + x2 * (-1.0 / 720 + x2 * (1.0 / 40320 - x2 * (1.0 / 3628800)))))
    c0 = jnp.where(swap, sin_x, cos_x)
    s0 = jnp.where(swap, cos_x, sin_x)
    cos_t = jnp.where(quad == 0, c0, jnp.where(quad == 1, -s0, jnp.where(quad == 2, -c0, s0)))
    sin_t = jnp.where(quad == 0, s0, jnp.where(quad == 1, c0, jnp.where(quad == 2, -s0, -c0)))
    return cos_t.astype(BF16), sin_t.astype(BF16)


def _hyena_tables(n):
    fwd = _parity_tables(n, 2 * n)
    return fwd, tuple(t.T for t in fwd)


def _parity_tables(n, period):
    k = jnp.arange(n // 2)
    ce, se = _dft_tables(k, 2 * k, period)
    co, so = _dft_tables(k, 2 * k + 1, period)
    return ce, co, se, so


def _grid_pos_embed(n_tokens, d):
    rows = n_tokens // GRID_W
    rr, cc = jnp.meshgrid(jnp.arange(rows, dtype=F32), jnp.arange(GRID_W, dtype=F32), indexing='ij')
    rr = rr.reshape(-1)[:, None]
    cc = cc.reshape(-1)[:, None]
    quarter = d // 4
    omega = 1.0 / (POS_BASE ** (jnp.arange(quarter, dtype=F32) / quarter))
    ar, ac = rr * omega, cc * omega
    return jnp.concatenate([jnp.sin(ar), jnp.cos(ar), jnp.sin(ac), jnp.cos(ac)], axis=-1)


def _hyena_pos_features(n):
    t_idx = jnp.arange(n, dtype=F32)
    t = jnp.linspace(0.0, 1.0, n, dtype=F32)
    bands = jnp.linspace(1e-4, HY_BANDS - 1, HY_BANDS, dtype=F32)
    ang = (2.0 * math.pi / n) * t_idx[:, None] * bands[None, :]
    z = jnp.concatenate([t[:, None], jnp.cos(ang), jnp.sin(ang)], axis=-1)
    return jnp.pad(z, ((0, 0), (0, LANE - z.shape[1])))


def _pad2(a, rows, cols):
    return jnp.pad(a, ((0, rows - a.shape[0]), (0, cols - a.shape[1])))


def kernel(x_prompt, x_sample, state_ret_fwd, state_ret_bwd, c, c_ctx, norm_g, mod_w, mod_b, ffn_a_in, ffn_a_out, ffn_b_in, ffn_b_out, ev_in_w, ev_out_w, fnet_w, hy_conv_w, hy_conv_b, hy_w1, hy_b1, hy_w2, hy_b2, hy_w_out, hy_freq, hy_decay, hy_bias, od_in_w, od_out_w, ret_log_decay_fwd, ret_log_decay_bwd, ret_gn, pool_w, pool_scale, final_norm):
    depth, d = norm_g.shape[0], norm_g.shape[2]
    half = d // 2
    dec_b = x_sample.shape[0]
    assert dec_b + 1 <= MOD_ROWS

    cond = jnp.concatenate([c, c_ctx[None, :]], axis=0)
    cond = jnp.pad(cond, ((0, MOD_ROWS - cond.shape[0]), (0, 0)))
    mod = _mod_call(cond, mod_w, mod_b).reshape(depth, MOD_ROWS, N_MOD, 1, d)

    bf = lambda a: a.astype(BF16)
    ffn_a_in, ffn_a_out, ffn_b_in, ffn_b_out = bf(ffn_a_in), bf(ffn_a_out), bf(ffn_b_in), bf(ffn_b_out)
    ev_in_w, ev_out_w, od_in_w, od_out_w = bf(ev_in_w), bf(ev_out_w), bf(od_in_w), bf(od_out_w)
    fnet_w, pool_w = bf(fnet_w), bf(pool_w)

    hid = hy_w1.shape[2]
    n_even = hy_w1.shape[0]
    hy_w1p = jnp.stack([_pad2(hy_w1[e], LANE, LANE) for e in range(n_even)])
    hy_w2p = jnp.stack([_pad2(hy_w2[e], LANE, LANE) for e in range(n_even)])
    hy_woutp = jnp.pad(hy_w_out, ((0, 0), (0, LANE - hid), (0, 0)))
    padv = lambda a: jnp.pad(a, ((0, 0), (0, LANE - hid)))[:, None, :]
    hy_b1p, hy_b2p, hy_freqp = padv(hy_b1), padv(hy_b2), padv(hy_freq)

    gc = half // N_GROUPS
    cs_chan = jnp.concatenate(_dft_tables(jnp.arange(gc), jnp.arange(gc), gc), axis=1)

    def trunk(x3, mod_rows, s0_f, s0_b, pos=None):
        b, n, _ = x3.shape
        t = b * n
        rows_per_mod = n if mod_rows.stop - mod_rows.start > 1 else t
        x = x3.reshape(t, d)
        hy_fwd_t, hy_inv_t = _hyena_tables(n)
        fnet_t = _parity_tables(n, n)
        z_pos = _hyena_pos_features(n)
        s_f_out, s_b_out = [], []
        for l in range(depth):
            m = [mod[l, mod_rows, i] for i in range(N_MOD)]
            x = _ffn_call(x, norm_g[l, 0][None], m[0], m[1], m[2], ffn_a_in, ffn_a_out, l, rows_per_mod,
                          pos=pos if l == 0 else None)
            if l % 2 == 0:
                e = l // 2
                u3 = _inproj_call(x, norm_g[l, 1][None], m[3], m[4], ev_in_w, e, rows_per_mod).reshape(b, n, -1)
                ya = _fnet_seq_call(fnet_t, _fnet_chan_call(u3, cs_chan, half), fnet_w[e])
                yb = _hyena_mix(u3, half, half, hy_fwd_t, hy_inv_t, z_pos, hy_conv_w[e], hy_conv_b[e][None],
                                hy_w1p[e], hy_b1p[e], hy_w2p[e], hy_b2p[e], hy_freqp[e], hy_woutp[e],
                                hy_decay[e][None], hy_bias[e][None])
                w_out, w_idx = ev_out_w, e
            else:
                o = l // 2
                u3 = _inproj_call(x, norm_g[l, 1][None], m[3], m[4], od_in_w, o, rows_per_mod).reshape(b, n, -1)
                ya, s_f, s_b = _ret_call(u3, half, ret_log_decay_fwd[o], ret_log_decay_bwd[o], ret_gn[o][None],
                                         s0_f[:, o], s0_b[:, o])
                yb = _pool_call(u3, 4 * half, half, pool_w[o], pool_scale[o][None])
                s_f_out.append(s_f)
                s_b_out.append(s_b)
                w_out, w_idx = od_out_w, o
            x = _outproj_call(x, ya.reshape(t, half), yb.reshape(t, half), w_out, w_idx, m[5], rows_per_mod)
            x = _ffn_call(x, norm_g[l, 2][None], m[6], m[7], m[8], ffn_b_in, ffn_b_out, l, rows_per_mod,
                          final_g=final_norm[None] if l == depth - 1 else None)
        return x.reshape(b, n, d), jnp.stack(s_f_out, axis=1), jnp.stack(s_b_out, axis=1)

    zero_state = jnp.zeros((x_prompt.shape[0],) + state_ret_fwd.shape[1:], F32)
    y_prompt, st_f, st_b = trunk(x_prompt, slice(dec_b, dec_b + 1), zero_state, zero_state)

    y_sample, _, _ = trunk(x_sample, slice(0, dec_b), state_ret_fwd, state_ret_bwd,
                           pos=_grid_pos_embed(x_sample.shape[1], d))
    return (y_prompt, y_sample, st_f, st_b)
```

```python
import functools
import math

import jax
import jax.numpy as jnp
from jax import lax
from jax.experimental import pallas as pl
from jax.experimental.pallas import tpu as pltpu

F32 = jnp.float32
BF16 = jnp.bfloat16

EPS = 1e-6
N_MOD = 9
GRID_W = 64
POS_BASE = 10000.0
N_GROUPS = 4
HY_BANDS = 16
RET_CHUNK = 256
POOL_WINDOWS = (2, 4, 8, 16)
POOL_PAD = 8
LANE = 128
MOD_ROWS = 16
VMEM_LIMIT = 56 * 1024 * 1024
NORM_ROWS = 32
NORM_UNROLL = 4
FFN_TM = 1024
FFN_TF = 512


def _params(*sem):
    return pltpu.CompilerParams(dimension_semantics=sem, vmem_limit_bytes=VMEM_LIMIT)


def _tile(n, pref):
    if n <= pref:
        return n
    t = pref
    while n % t:
        t -= LANE
    assert t > 0, (n, pref)
    return t


def _silu(x):
    return x * (1.0 / (1.0 + jnp.exp(-x)))


def _norm_mod_into(h_ref, x_ref, g_ref, shift_ref, scale_ref):
    gs = g_ref[...] * (1.0 + scale_ref[...])
    shift = shift_ref[...]

    def body(r, carry):
        rows = pl.ds(pl.multiple_of(r * NORM_ROWS, NORM_ROWS), NORM_ROWS)
        x = x_ref[rows, :]
        ms = jnp.mean(x * x, axis=-1, keepdims=True)
        h_ref[rows, :] = (x * lax.rsqrt(ms + EPS) * gs + shift).astype(h_ref.dtype)
        return carry

    lax.fori_loop(0, x_ref.shape[0] // NORM_ROWS, body, 0, unroll=NORM_UNROLL)


def _dot(a, b):
    return jnp.dot(a, b, preferred_element_type=F32)


def _mod_body(c_ref, w_ref, b_ref, o_ref):
    s = _silu(c_ref[...]).astype(BF16)
    o_ref[...] = _dot(s, w_ref[...].astype(BF16)) + b_ref[...]


def _mod_call(cond, mod_w, mod_b):
    depth, d, n_out = mod_w.shape
    tn = _tile(n_out, 1024)
    return pl.pallas_call(
        _mod_body,
        grid=(depth, n_out // tn),
        in_specs=[
            pl.BlockSpec((MOD_ROWS, d), lambda l, j: (0, 0)),
            pl.BlockSpec((None, d, tn), lambda l, j: (l, 0, j)),
            pl.BlockSpec((None, 1, tn), lambda l, j: (l, 0, j)),
        ],
        out_specs=pl.BlockSpec((None, MOD_ROWS, tn), lambda l, j: (l, 0, j)),
        out_shape=jax.ShapeDtypeStruct((depth, MOD_ROWS, n_out), F32),
        compiler_params=_params("parallel", "parallel"),
        name="mod_proj",
    )(cond, mod_w, mod_b.reshape(depth, 1, n_out))


def _ffn_body(final, with_pos, x_ref, g_ref, sh_ref, sc_ref, gt_ref, wg_ref, wu_ref, wo_ref, *rest):
    rest = list(rest)
    pos_ref = rest.pop(0) if with_pos else None
    fg_ref = rest.pop(0) if final else None
    o_ref, h_ref = rest
    j = pl.program_id(1)

    @pl.when(j == 0)
    def _():
        if with_pos:
            o_ref[...] = x_ref[...] + pos_ref[...]
            _norm_mod_into(h_ref, o_ref, g_ref, sh_ref, sc_ref)
        else:
            _norm_mod_into(h_ref, x_ref, g_ref, sh_ref, sc_ref)
            o_ref[...] = x_ref[...]

    h = h_ref[...]
    gate = _dot(h, wg_ref[...])
    up = _dot(h, wu_ref[...])
    a = (_silu(gate) * up).astype(BF16)
    o_ref[...] += (0.5 * gt_ref[...]) * _dot(a, wo_ref[...])

    if final:
        @pl.when(j == pl.num_programs(1) - 1)
        def _():
            _rmsnorm_rows(o_ref, fg_ref)


def _rmsnorm_rows(o_ref, g_ref):
    g = g_ref[...]

    def body(r, carry):
        rows = pl.ds(pl.multiple_of(r * NORM_ROWS, NORM_ROWS), NORM_ROWS)
        y = o_ref[rows, :]
        ms = jnp.mean(y * y, axis=-1, keepdims=True)
        o_ref[rows, :] = y * lax.rsqrt(ms + EPS) * g
        return carry

    lax.fori_loop(0, o_ref.shape[0] // NORM_ROWS, body, 0, unroll=NORM_UNROLL)


def _ffn_call(x, g, shift, scale, gate, w_in, w_out, layer, rows_per_mod, final_g=None, pos=None):
    t, d = x.shape
    d_ff = w_out.shape[1]
    tm = _tile(rows_per_mod, FFN_TM // 2 if pos is not None else FFN_TM)
    tf = _tile(d_ff, FFN_TF)
    nf = d_ff // tf
    final = final_g is not None
    with_pos = pos is not None
    tiles_per_mod = rows_per_mod // tm
    mod_spec = pl.BlockSpec((None, 1, d), lambda i, j: (i // tiles_per_mod, 0, 0))
    vec_spec = pl.BlockSpec((1, d), lambda i, j: (0, 0))
    in_specs = [
        pl.BlockSpec((tm, d), lambda i, j: (i, 0)),
        vec_spec, mod_spec, mod_spec, mod_spec,
        pl.BlockSpec((None, d, tf), lambda i, j: (layer, 0, j)),
        pl.BlockSpec((None, d, tf), lambda i, j: (layer, 0, j + nf)),
        pl.BlockSpec((None, tf, d), lambda i, j: (layer, j, 0)),
    ]
    args = [x, g, shift, scale, gate, w_in, w_in, w_out]
    if with_pos:
        in_specs.append(pl.BlockSpec((tm, d), lambda i, j: (i % tiles_per_mod, 0)))
        args.append(pos)
    if final:
        in_specs.append(vec_spec)
        args.append(final_g)
    return pl.pallas_call(
        functools.partial(_ffn_body, final, with_pos),
        grid=(t // tm, nf),
        in_specs=in_specs,
        out_specs=pl.BlockSpec((tm, d), lambda i, j: (i, 0)),
        out_shape=jax.ShapeDtypeStruct((t, d), F32),
        scratch_shapes=[pltpu.VMEM((tm, d), BF16)],
        compiler_params=_params("parallel", "arbitrary"),
        name="ffn",
    )(*args)


def _inproj_body(x_ref, g_ref, sh_ref, sc_ref, w_ref, o_ref, h_ref):
    @pl.when(pl.program_id(1) == 0)
    def _():
        _norm_mod_into(h_ref, x_ref, g_ref, sh_ref, sc_ref)

    o_ref[...] = _dot(h_ref[...], w_ref[...]).astype(BF16)


def _inproj_call(x, g, shift, scale, w, layer, rows_per_mod):
    t, d = x.shape
    n_out = w.shape[2]
    tm = _tile(rows_per_mod, 1024)
    tn = _tile(n_out, 1024)
    mod_spec = pl.BlockSpec((None, 1, d), lambda i, j: ((i * tm) // rows_per_mod, 0, 0))
    return pl.pallas_call(
        _inproj_body,
        grid=(t // tm, n_out // tn),
        in_specs=[
            pl.BlockSpec((tm, d), lambda i, j: (i, 0)),
            pl.BlockSpec((1, d), lambda i, j: (0, 0)),
            mod_spec, mod_spec,
            pl.BlockSpec((None, d, tn), lambda i, j: (layer, 0, j)),
        ],
        out_specs=pl.BlockSpec((tm, tn), lambda i, j: (i, j)),
        out_shape=jax.ShapeDtypeStruct((t, n_out), BF16),
        scratch_shapes=[pltpu.VMEM((tm, d), BF16)],
        compiler_params=_params("parallel", "arbitrary"),
        name="mix_in_proj",
    )(x, g, shift, scale, w)


def _outproj_body(x_ref, ya_ref, yb_ref, wa_ref, wb_ref, gt_ref, o_ref):
    y = _dot(ya_ref[...], wa_ref[...]) + _dot(yb_ref[...], wb_ref[...])
    o_ref[...] = x_ref[...] + gt_ref[...] * y


def _outproj_call(x, ya, yb, w, layer, gate, rows_per_mod):
    t, d = x.shape
    half = ya.shape[1]
    tm = _tile(rows_per_mod, 512)
    return pl.pallas_call(
        _outproj_body,
        grid=(t // tm,),
        in_specs=[
            pl.BlockSpec((tm, d), lambda i: (i, 0)),
            pl.BlockSpec((tm, half), lambda i: (i, 0)),
            pl.BlockSpec((tm, half), lambda i: (i, 0)),
            pl.BlockSpec((None, half, d), lambda i: (layer, 0, 0)),
            pl.BlockSpec((None, half, d), lambda i: (layer, 1, 0)),
            pl.BlockSpec((None, 1, d), lambda i: ((i * tm) // rows_per_mod, 0, 0)),
        ],
        out_specs=pl.BlockSpec((tm, d), lambda i: (i, 0)),
        out_shape=jax.ShapeDtypeStruct((t, d), F32),
        compiler_params=_params("parallel"),
        name="mix_out_proj",
    )(x, ya, yb, w, w, gate)


def _fnet_chan_body(gc, x_ref, cs_ref, ze_ref, zo_ref, z_scr):
    half = N_GROUPS * gc
    tr = x_ref.shape[0]

    def park(col0, val):
        done = 0
        while done < val.shape[1]:
            slab, lane0 = divmod(col0 + done, LANE)
            w = min(LANE - lane0, val.shape[1] - done)
            z_scr[slab, :, lane0:lane0 + w] = val[:, done:done + w]
            done += w

    for g in range(N_GROUPS):
        xg = x_ref[:, g * gc:(g + 1) * gc].astype(BF16)
        z = _dot(xg, cs_ref[...])
        park(g * gc, z[:, :gc])
        park(half + g * gc, z[:, gc:])
    for slab in range(z_scr.shape[0]):
        cols = slice(slab * LANE, (slab + 1) * LANE)
        ze_ref[:, cols] = z_scr[slab, pl.ds(0, tr // 2, stride=2), :].astype(BF16)
        zo_ref[:, cols] = z_scr[slab, pl.ds(1, tr // 2, stride=2), :].astype(BF16)


def _fnet_chan_call(u3, cs, width):
    b, n, _ = u3.shape
    gc = width // N_GROUPS
    tr = _tile(n, 512)
    assert tr % 2 == 0 and (2 * width) % LANE == 0
    out = pl.BlockSpec((None, tr // 2, 2 * width), lambda bi, i: (bi, i, 0))
    return pl.pallas_call(
        functools.partial(_fnet_chan_body, gc),
        grid=(b, n // tr),
        in_specs=[
            pl.BlockSpec((None, tr, width), lambda bi, i: (bi, i, 0)),
            pl.BlockSpec((gc, 2 * gc), lambda bi, i: (0, 0)),
        ],
        out_specs=[out, out],
        out_shape=[jax.ShapeDtypeStruct((b, n // 2, 2 * width), BF16)] * 2,
        scratch_shapes=[pltpu.VMEM((2 * width // LANE, tr, LANE), F32)],
        compiler_params=_params("parallel", "parallel"),
        name="fnet_chan_dft",
    )(u3, cs)


def _fnet_seq_body(gc, scale, ce_ref, co_ref, se_ref, so_ref, ze_ref, zo_ref, w_ref, o_ref, e_ref, od_ref):
    k = pl.program_id(2)
    width = N_GROUPS * gc

    @pl.when(k == 0)
    def _():
        e_ref[...] = jnp.zeros_like(e_ref)
        od_ref[...] = jnp.zeros_like(od_ref)

    e_ref[...] += _dot(ce_ref[...], ze_ref[:, :width]) - _dot(se_ref[...], ze_ref[:, width:])
    od_ref[...] += _dot(co_ref[...], zo_ref[:, :width]) - _dot(so_ref[...], zo_ref[:, width:])

    @pl.when(k == pl.num_programs(2) - 1)
    def _():
        for g in range(N_GROUPS):
            sl = slice(g * gc, (g + 1) * gc)
            even, odd = e_ref[:, sl], od_ref[:, sl]
            o_ref[0, :, sl] = _dot(((even + odd) * scale).astype(BF16), w_ref[g]).astype(BF16)
            o_ref[1, :, sl] = _dot(((even - odd) * scale).astype(BF16), w_ref[g]).astype(BF16)


def _fnet_seq_call(tables, z_even, z_odd, w):
    b, h, two_w = z_even.shape
    width = two_w // 2
    n = 2 * h
    gc = width // N_GROUPS
    tm = _tile(h, 1024)
    tk = _tile(h, 512)
    scale = 1.0 / math.sqrt(n * gc)
    mat = pl.BlockSpec((tm, tk), lambda bi, i, k: (i, k))
    out = pl.pallas_call(
        functools.partial(_fnet_seq_body, gc, scale),
        grid=(b, h // tm, h // tk),
        in_specs=[mat, mat, mat, mat,
                  pl.BlockSpec((None, tk, two_w), lambda bi, i, k: (bi, k, 0)),
                  pl.BlockSpec((None, tk, two_w), lambda bi, i, k: (bi, k, 0)),
                  pl.BlockSpec((N_GROUPS, gc, gc), lambda bi, i, k: (0, 0, 0))],
        out_specs=pl.BlockSpec((None, 2, tm, width), lambda bi, i, k: (bi, 0, i, 0)),
        out_shape=jax.ShapeDtypeStruct((b, 2, h, width), BF16),
        scratch_shapes=[pltpu.VMEM((tm, width), F32), pltpu.VMEM((tm, width), F32)],
        compiler_params=_params("parallel", "parallel", "arbitrary"),
        name="fnet_seq_dft",
    )(*tables, z_even, z_odd, w)
    return out.reshape(b, n, width)


def _alt_sums(x, row):
    phase = row % 4
    even = jnp.sum(jnp.where(phase == 0, x, jnp.where(phase == 2, -x, 0.0)), axis=0, keepdims=True)
    odd = jnp.sum(jnp.where(phase == 1, x, jnp.where(phase == 3, -x, 0.0)), axis=0, keepdims=True)
    return even, odd


def _hy_pre_body(n, u0_ref, u1_ref, u2_ref, w0_ref, w1_ref, w2_ref, b0_ref, b1_ref, b2_ref,
                 x0e_ref, x0o_ref, vve_ref, vvo_ref, me_ref, mo_ref, x0_scr, vv_scr):
    row = lax.broadcasted_iota(jnp.int32, u0_ref.shape, 0)
    h = n // 2

    def conv(u_ref, w_ref, b_ref):
        u = u_ref[...].astype(F32)
        prev = jnp.where(row == 0, 0.0, pltpu.roll(u, 1, 0))
        nxt = jnp.where(row == n - 1, 0.0, pltpu.roll(u, n - 1, 0))
        return prev * w_ref[0:1, :] + u * w_ref[1:2, :] + nxt * w_ref[2:3, :] + b_ref[...]

    x0_scr[...] = conv(u0_ref, w0_ref, b0_ref)
    vv = (conv(u2_ref, w2_ref, b2_ref) * conv(u1_ref, w1_ref, b1_ref)).astype(BF16).astype(F32)
    vv_scr[...] = vv
    me_ref[...], mo_ref[...] = _alt_sums(vv, row)
    even, odd = pl.ds(0, h, stride=2), pl.ds(1, h, stride=2)
    x0e_ref[...] = x0_scr[even, :].astype(BF16)
    x0o_ref[...] = x0_scr[odd, :].astype(BF16)
    vve_ref[...] = vv_scr[even, :].astype(BF16)
    vvo_ref[...] = vv_scr[odd, :].astype(BF16)


def _hy_pre_call(u3, col0, width, conv_w, conv_b):
    b, n, _ = u3.shape
    tc = _tile(width, LANE)
    nct = width // tc

    def u_spec(s):
        return pl.BlockSpec((None, n, tc), lambda bi, c: (bi, 0, (col0 + s * width) // tc + c))

    def w_spec(s):
        return pl.BlockSpec((3, tc), lambda bi, c: (0, s * nct + c))

    def b_spec(s):
        return pl.BlockSpec((1, tc), lambda bi, c: (0, s * nct + c))

    out3 = pl.BlockSpec((None, n // 2, tc), lambda bi, c: (bi, 0, c))
    vec = pl.BlockSpec((None, 1, tc), lambda bi, c: (bi, 0, c))
    halfseq = jax.ShapeDtypeStruct((b, n // 2, width), BF16)
    return pl.pallas_call(
        functools.partial(_hy_pre_body, n),
        grid=(b, nct),
        in_specs=[u_spec(0), u_spec(1), u_spec(2), w_spec(0), w_spec(1), w_spec(2),
                  b_spec(0), b_spec(1), b_spec(2)],
        out_specs=[out3, out3, out3, out3, vec, vec],
        out_shape=[halfseq, halfseq, halfseq, halfseq,
                   jax.ShapeDtypeStruct((b, 1, width), F32),
                   jax.ShapeDtypeStruct((b, 1, width), F32)],
        scratch_shapes=[pltpu.VMEM((n, tc), F32), pltpu.VMEM((n, tc), F32)],
        compiler_params=_params("parallel", "parallel"),
        name="hyena_short_conv",
    )(u3, u3, u3, conv_w, conv_w, conv_w, conv_b, conv_b, conv_b)


def _hy_filter_body(z_ref, w1_ref, b1_ref, w2_ref, b2_ref, fr_ref, wf_ref, wb_ref, dec_ref,
                    hs_ref, hd_ref, me_ref, mo_ref):
    hp = lax.Precision.HIGHEST
    z = z_ref[...]
    fr = fr_ref[...]
    h = jnp.sin(fr * (jnp.dot(z, w1_ref[...], precision=hp, preferred_element_type=F32) + b1_ref[...]))
    h = jnp.sin(fr * (jnp.dot(h, w2_ref[...], precision=hp, preferred_element_type=F32) + b2_ref[...]))
    window = jnp.exp(-z[:, 0:1] * jnp.abs(dec_ref[...]))
    hf = jnp.dot(h, wf_ref[...], precision=hp, preferred_element_type=F32) * window
    hb = jnp.dot(h, wb_ref[...], precision=hp, preferred_element_type=F32) * window
    row = lax.broadcasted_iota(jnp.int32, hf.shape, 0)
    hb = jnp.where(row == 0, 0.0, hb)
    l1 = jnp.sum(jnp.abs(hf), axis=0, keepdims=True) + jnp.sum(jnp.abs(hb), axis=0, keepdims=True)
    hs = ((hf + hb) / l1).astype(BF16)
    hd = ((hb - hf) / l1).astype(BF16)
    hs_ref[...] = hs
    hd_ref[...] = hd
    me_ref[...] = _alt_sums(hs.astype(F32), row)[0]
    mo_ref[...] = _alt_sums(hd.astype(F32), row)[1]


def _hy_filter_call(z, w1, b1, w2, b2, freq, w_out, decay):
    n, kz = z.shape
    hid = w1.shape[1]
    width = decay.shape[1]
    tc = _tile(width, 256)
    nct = width // tc
    full = lambda a: pl.BlockSpec(a.shape, lambda c: (0, 0))
    col = pl.BlockSpec((n, tc), lambda c: (0, c))
    vec = pl.BlockSpec((1, tc), lambda c: (0, c))
    return pl.pallas_call(
        _hy_filter_body,
        grid=(nct,),
        in_specs=[full(z), full(w1), full(b1), full(w2), full(b2), full(freq),
                  pl.BlockSpec((hid, tc), lambda c: (0, c)),
                  pl.BlockSpec((hid, tc), lambda c: (0, c + nct)),
                  vec],
        out_specs=[col, col, vec, vec],
        out_shape=[jax.ShapeDtypeStruct((n, width), BF16),
                   jax.ShapeDtypeStruct((n, width), BF16),
                   jax.ShapeDtypeStruct((1, width), F32),
                   jax.ShapeDtypeStruct((1, width), F32)],
        compiler_params=_params("parallel"),
        name="hyena_filter",
    )(z, w1, b1, w2, b2, freq, w_out, w_out, decay)


def _hy_zero(*refs):
    for r in refs:
        r[...] = jnp.zeros_like(r)


def _hy_spec_body(width, ce_ref, co_ref, se_ref, so_ref, hs_ref, hd_ref, t_ref, a_ref, b_ref, c_ref, d_ref):
    @pl.when(pl.program_id(1) == 0)
    def _():
        _hy_zero(a_ref, b_ref, c_ref, d_ref)

    a_ref[...] += _dot(ce_ref[...], hs_ref[:, :width])
    b_ref[...] += _dot(co_ref[...], hs_ref[:, width:])
    c_ref[...] += _dot(se_ref[...], hd_ref[:, :width])
    d_ref[...] += _dot(so_ref[...], hd_ref[:, width:])

    @pl.when(pl.program_id(1) == pl.num_programs(1) - 1)
    def _():
        a, b, c, d = a_ref[...], b_ref[...], c_ref[...], d_ref[...]
        t_ref[:, 0 * width:1 * width] = a + b
        t_ref[:, 1 * width:2 * width] = c + d
        t_ref[:, 2 * width:3 * width] = a - b
        t_ref[:, 3 * width:4 * width] = d - c


def _hy_spec_call(tables, hs2, hd2):
    h, two_w = hs2.shape
    width = two_w // 2
    tm = _tile(h, 512)
    tk = _tile(h, 512)
    mat = pl.BlockSpec((tm, tk), lambda i, k: (i, k))
    rhs = pl.BlockSpec((tk, two_w), lambda i, k: (k, 0))
    return pl.pallas_call(
        functools.partial(_hy_spec_body, width),
        grid=(h // tm, h // tk),
        in_specs=[mat, mat, mat, mat, rhs, rhs],
        out_specs=pl.BlockSpec((tm, 4 * width), lambda i, k: (i, 0)),
        out_shape=jax.ShapeDtypeStruct((h, 4 * width), F32),
        scratch_shapes=[pltpu.VMEM((tm, width), F32)] * 4,
        compiler_params=_params("parallel", "arbitrary"),
        name="hyena_filter_spectrum",
    )(*tables, hs2, hd2)


def _hy_fwd_body(n, width, ce_ref, co_ref, se_ref, so_ref, ve_ref, vo_ref, t_ref, g_ref,
                 a_ref, b_ref, c_ref, d_ref):
    i = pl.program_id(1)
    k = pl.program_id(2)

    @pl.when(k == 0)
    def _():
        _hy_zero(a_ref, b_ref, c_ref, d_ref)

    even, odd = ve_ref[...], vo_ref[...]
    a_ref[...] += _dot(ce_ref[...], even)
    b_ref[...] += _dot(co_ref[...], odd)
    c_ref[...] += _dot(se_ref[...], even)
    d_ref[...] += _dot(so_ref[...], odd)

    @pl.when(k == pl.num_programs(2) - 1)
    def _():
        tm = a_ref.shape[0]
        freq = lax.broadcasted_iota(jnp.int32, (tm, 1), 0) + i * tm
        wk = jnp.where(freq == 0, 0.5 / n, 1.0 / n)
        a, b, c, d = a_ref[...], b_ref[...], c_ref[...], d_ref[...]

        def product(v_cos, v_sin, t_re, t_im):
            return (v_cos * t_re + v_sin * t_im) * wk, (v_cos * t_im - v_sin * t_re) * wk

        re_lo, im_lo = product(a + b, c + d, t_ref[:, 0 * width:1 * width], t_ref[:, 1 * width:2 * width])
        re_hi, im_hi = product(a - b, d - c, t_ref[:, 2 * width:3 * width], t_ref[:, 3 * width:4 * width])
        g_ref[:, 0 * width:1 * width] = (re_lo + re_hi).astype(BF16)
        g_ref[:, 1 * width:2 * width] = (im_lo - im_hi).astype(BF16)
        g_ref[:, 2 * width:3 * width] = (re_lo - re_hi).astype(BF16)
        g_ref[:, 3 * width:4 * width] = (im_lo + im_hi).astype(BF16)


def _hy_fwd_call(tables, v_even, v_odd, t_spec):
    b, h, width = v_even.shape
    tm = _tile(h, 512)
    tk = _tile(h, 512)
    mat = pl.BlockSpec((tm, tk), lambda bi, i, k: (i, k))
    return pl.pallas_call(
        functools.partial(_hy_fwd_body, 2 * h, width),
        grid=(b, h // tm, h // tk),
        in_specs=[mat, mat, mat, mat,
                  pl.BlockSpec((None, tk, width), lambda bi, i, k: (bi, k, 0)),
                  pl.BlockSpec((None, tk, width), lambda bi, i, k: (bi, k, 0)),
                  pl.BlockSpec((tm, 4 * width), lambda bi, i, k: (i, 0))],
        out_specs=pl.BlockSpec((None, tm, 4 * width), lambda bi, i, k: (bi, i, 0)),
        out_shape=jax.ShapeDtypeStruct((b, h, 4 * width), BF16),
        scratch_shapes=[pltpu.VMEM((tm, width), F32)] * 4,
        compiler_params=_params("parallel", "parallel", "arbitrary"),
        name="hyena_fwd_dft",
    )(*tables, v_even, v_odd, t_spec)


def _hy_inv_body(n, width, ce_ref, co_ref, se_ref, so_ref, g_ref, x0e_ref, x0o_ref, vve_ref, vvo_ref,
                 vme_ref, vmo_ref, tme_ref, tmo_ref, bias_ref, o_ref, e_ref, od_ref):
    k = pl.program_id(2)

    @pl.when(k == 0)
    def _():
        _hy_zero(e_ref, od_ref)

    e_ref[...] += _dot(ce_ref[...], g_ref[:, 0 * width:1 * width]) - _dot(se_ref[...], g_ref[:, 1 * width:2 * width])
    od_ref[...] += _dot(co_ref[...], g_ref[:, 2 * width:3 * width]) - _dot(so_ref[...], g_ref[:, 3 * width:4 * width])

    @pl.when(k == pl.num_programs(2) - 1)
    def _():
        tm = e_ref.shape[0]
        sign = jnp.where(lax.broadcasted_iota(jnp.int32, (tm, 1), 0) % 2 == 0, 1.0, -1.0)
        v_e, v_o, t_e, t_o = vme_ref[...], vmo_ref[...], tme_ref[...], tmo_ref[...]
        mid_re = (v_e * t_e + v_o * t_o) * (1.0 / n)
        mid_im = (v_e * t_o - v_o * t_e) * (1.0 / n)
        conv_even = e_ref[...] + sign * mid_re
        conv_odd = od_ref[...] - sign * mid_im
        bias = bias_ref[...]

        def finish(cols, conv, x0_ref, vv_ref):
            o_ref[:, cols] = (x0_ref[...].astype(F32) * (conv + vv_ref[...].astype(F32) * bias)).astype(BF16)

        finish(slice(0, width), conv_even, x0e_ref, vve_ref)
        finish(slice(width, 2 * width), conv_odd, x0o_ref, vvo_ref)


def _hy_inv_call(tables, g, x0_even, x0_odd, v_even, v_odd, vm_e, vm_o, tm_e, tm_o, bias):
    b, h, width = v_even.shape
    two_w = 2 * width
    tm = _tile(h, 512)
    tk = _tile(h, 512)
    assert tm % 2 == 0
    mat = pl.BlockSpec((tm, tk), lambda bi, i, k: (i, k))
    tok = pl.BlockSpec((None, tm, two_w), lambda bi, i, k: (bi, i, 0))
    par = pl.BlockSpec((None, tm, width), lambda bi, i, k: (bi, i, 0))
    bvec = pl.BlockSpec((None, 1, width), lambda bi, i, k: (bi, 0, 0))
    vec = pl.BlockSpec((1, width), lambda bi, i, k: (0, 0))
    return pl.pallas_call(
        functools.partial(_hy_inv_body, 2 * h, width),
        grid=(b, h // tm, h // tk),
        in_specs=[mat, mat, mat, mat,
                  pl.BlockSpec((None, tk, 4 * width), lambda bi, i, k: (bi, k, 0)),
                  par, par, par, par, bvec, bvec, vec, vec, vec],
        out_specs=tok,
        out_shape=jax.ShapeDtypeStruct((b, h, two_w), BF16),
        scratch_shapes=[pltpu.VMEM((tm, width), F32)] * 2,
        compiler_params=_params("parallel", "parallel", "arbitrary"),
        name="hyena_inv_dft",
    )(*tables, g, x0_even, x0_odd, v_even, v_odd, vm_e, vm_o, tm_e, tm_o, bias)


def _hyena_mix(u3, col0, width, fwd_tables, inv_tables, z_pos, conv_w, conv_b, w1, b1, w2, b2, freq, w_out,
               decay, bias):
    b, n, _ = u3.shape
    fold = lambda a: a.reshape(a.shape[:-2] + (n // 2, 2 * width))
    x0_e, x0_o, v_e, v_o, vm_e, vm_o = _hy_pre_call(u3, col0, width, conv_w, conv_b)
    hs, hd, tm_e, tm_o = _hy_filter_call(z_pos, w1, b1, w2, b2, freq, w_out, decay)
    t_spec = _hy_spec_call(fwd_tables, fold(hs), fold(hd))
    g = _hy_fwd_call(fwd_tables, v_e, v_o, t_spec)
    y2 = _hy_inv_call(inv_tables, g, x0_e, x0_o, v_e, v_o, vm_e, vm_o, tm_e, tm_o, bias)
    return y2.reshape(b, n, width)


def _ret_body(dk, ch, nc, lgf_ref, lgb_ref, q_ref, k_ref, v_ref, g_ref, gn_ref, s0f_ref, s0b_ref,
              y_ref, sfo_ref, sbo_ref, o_ref, sf_ref, sb_ref):
    h = pl.program_id(1)
    sf_ref[...] = s0f_ref[...]
    sb_ref[...] = s0b_ref[...]

    ii = lax.broadcasted_iota(jnp.int32, (ch, ch), 0)
    jj = lax.broadcasted_iota(jnp.int32, (ch, ch), 1)
    diff = (ii - jj).astype(F32)
    idx = lax.broadcasted_iota(jnp.int32, (ch, 1), 0).astype(F32)
    k_scale = dk ** -0.5

    def decays(lg, lag, q_pow, k_pow):
        inner = jnp.where(lag >= 0, jnp.exp(lg * jnp.maximum(lag, 0.0)), 0.0)
        return inner, jnp.exp(lg * q_pow), jnp.exp(lg * k_pow), jnp.exp(lg * jnp.full((1, 1), float(ch), F32))

    dec_f = decays(lgf_ref[h], diff, idx + 1.0, (ch - 1.0) - idx)
    dec_b = decays(lgb_ref[h], -diff, float(ch) - idx, idx)

    def rows(c):
        return pl.ds(pl.multiple_of(c * ch, ch), ch)

    def chunk(c, dec, s_ref):
        inner_decay, q_decay, k_decay, chunk_decay = dec
        q = q_ref[rows(c), :]
        k = k_ref[rows(c), :].astype(F32) * k_scale
        v = v_ref[rows(c), :]
        scores = lax.dot_general(q, k.astype(BF16), (((1,), (1,)), ((), ())),
                                 preferred_element_type=F32) * inner_decay
        s = s_ref[...]
        out = _dot(scores.astype(BF16), v) + _dot(q, s.astype(BF16)) * q_decay
        kd = (k * k_decay).T.astype(BF16)
        s_ref[...] = s * chunk_decay + _dot(kd, v)
        return out

    def finish(c, o):
        mu = jnp.mean(o, axis=-1, keepdims=True)
        var = jnp.mean(jnp.square(o - mu), axis=-1, keepdims=True)
        on = (o - mu) * lax.rsqrt(var + EPS) * gn_ref[...]
        y_ref[rows(c), :] = (_silu(g_ref[rows(c), :].astype(F32)) * on).astype(BF16)

    def first_half(c, carry):
        cb = nc - 1 - c
        o_ref[rows(c), :] = chunk(c, dec_f, sf_ref)
        o_ref[rows(cb), :] = chunk(cb, dec_b, sb_ref)
        return carry

    def second_half(c, carry):
        cb = nc - 1 - c
        finish(c, o_ref[rows(c), :] + chunk(c, dec_f, sf_ref))
        finish(cb, o_ref[rows(cb), :] + chunk(cb, dec_b, sb_ref))
        return carry

    lax.fori_loop(0, nc // 2, first_half, 0)
    lax.fori_loop(nc // 2, nc, second_half, 0)
    sfo_ref[...] = sf_ref[...]
    sbo_ref[...] = sb_ref[...]


def _ret_call(u3, width, lgf, lgb, gn, s0f, s0b):
    b, n, _ = u3.shape
    dk = width // N_GROUPS
    ch = min(RET_CHUNK, n // 2)
    nc = n // ch
    assert nc % 2 == 0 and nc * ch == n

    def tok(part):
        return pl.BlockSpec((None, n, dk), lambda bi, h: (bi, 0, part * N_GROUPS + h))

    state = pl.BlockSpec((None, None, dk, dk), lambda bi, h: (bi, h, 0, 0))
    smem = pl.BlockSpec(memory_space=pltpu.SMEM)
    return pl.pallas_call(
        functools.partial(_ret_body, dk, ch, nc),
        grid=(b, N_GROUPS),
        in_specs=[smem, smem, tok(0), tok(1), tok(2), tok(3),
                  pl.BlockSpec((1, dk), lambda bi, h: (0, h)), state, state],
        out_specs=[pl.BlockSpec((None, n, dk), lambda bi, h: (bi, 0, h)), state, state],
        out_shape=[jax.ShapeDtypeStruct((b, n, width), BF16),
                   jax.ShapeDtypeStruct((b, N_GROUPS, dk, dk), F32),
                   jax.ShapeDtypeStruct((b, N_GROUPS, dk, dk), F32)],
        scratch_shapes=[pltpu.VMEM((n, dk), F32), pltpu.VMEM((dk, dk), F32), pltpu.VMEM((dk, dk), F32)],
        compiler_params=_params("parallel", "parallel"),
        name="retention",
    )(lgf, lgb, u3, u3, u3, u3, gn, s0f, s0b)


def _pool_body(n, x_ref, w_ref, sc_ref, y_ref):
    g = pl.program_id(1)
    gc = x_ref.shape[1]
    n_pad = n + 2 * POOL_PAD
    pos = lax.broadcasted_iota(jnp.int32, (n, 1), 0)

    for gi, win in enumerate(POOL_WINDOWS):
        @pl.when(g == gi)
        def _(win=win):
            x = x_ref[...].astype(F32)
            zeros = jnp.zeros((POOL_PAD, gc), F32)
            s = jnp.concatenate([zeros, x, zeros], axis=0)
            span = 1
            while span < win:
                s = s + pltpu.roll(s, span, 0)
                span *= 2
            total = pltpu.roll(s, n_pad - (POOL_PAD + win // 2 - 1), 0)[:n]
            lo = jnp.clip(pos - win // 2, 0, n - 1)
            hi = jnp.clip(pos - win // 2 + win - 1, 0, n - 1)
            count = (hi - lo + 1).astype(F32)
            p = total / count - x
            y = _dot(p.astype(BF16), w_ref[...]) * sc_ref[...]
            y_ref[...] = y.astype(BF16)


def _pool_call(u3, col0, width, w, scale):
    b, n, _ = u3.shape
    gc = width // N_GROUPS
    assert max(POOL_WINDOWS) // 2 <= POOL_PAD
    return pl.pallas_call(
        functools.partial(_pool_body, n),
        grid=(b, N_GROUPS),
        in_specs=[pl.BlockSpec((None, n, gc), lambda bi, g: (bi, 0, col0 // gc + g)),
                  pl.BlockSpec((None, gc, gc), lambda bi, g: (g, 0, 0)),
                  pl.BlockSpec((1, gc), lambda bi, g: (0, g))],
        out_specs=pl.BlockSpec((None, n, gc), lambda bi, g: (bi, 0, g)),
        out_shape=jax.ShapeDtypeStruct((b, n, width), BF16),
        compiler_params=_params("parallel", "parallel"),
        name="pool_mix",
    )(u3, w, scale)


def _dft_tables(row_vals, col_vals, period):
    assert period % 8 == 0
    k = row_vals.astype(jnp.int32)
    s = col_vals.astype(jnp.int32)
    m = (k[:, None] * s[None, :]) % period
    quarter = period // 4
    quad = m // quarter
    r = m % quarter
    swap = r > quarter // 2
    x = jnp.where(swap, quarter - r, r).astype(F32) * (2.0 * math.pi / period)
    x2 = x * x
    sin_x = x * (1.0 + x2 * (-1.0 / 6 + x2 * (1.0 / 120 + x2 * (-1.0 / 5040 + x2 * (1.0 / 362880)))))
    cos_x = 1.0 + x2 * (-0.5 + x2 * (1.0 / 24 + x2 * (-1.0 / 720 + x2 * (1.0 / 40320 - x2 * (1.0 / 3628800)))))
    c0 = jnp.where(swap, sin_x, cos_x)
    s0 = jnp.where(swap, cos_x, sin_x)
    cos_t = jnp.where(quad == 0, c0, jnp.where(quad == 1, -s0, jnp.where(quad == 2, -c0, s0)))
    sin_t = jnp.where(quad == 0, s0, jnp.where(quad == 1, c0, jnp.where(quad == 2, -s0, -c0)))
    return cos_t.astype(BF16), sin_t.astype(BF16)


def _hyena_tables(n):
    fwd = _parity_tables(n, 2 * n)
    return fwd, tuple(t.T for t in fwd)


def _parity_tables(n, period):
    k = jnp.arange(n // 2)
    ce, se = _dft_tables(k, 2 * k, period)
    co, so = _dft_tables(k, 2 * k + 1, period)
    return ce, co, se, so


def _grid_pos_embed(n_tokens, d):
    rows = n_tokens // GRID_W
    rr, cc = jnp.meshgrid(jnp.arange(rows, dtype=F32), jnp.arange(GRID_W, dtype=F32), indexing='ij')
    rr = rr.reshape(-1)[:, None]
    cc = cc.reshape(-1)[:, None]
    quarter = d // 4
    omega = 1.0 / (POS_BASE ** (jnp.arange(quarter, dtype=F32) / quarter))
    ar, ac = rr * omega, cc * omega
    return jnp.concatenate([jnp.sin(ar), jnp.cos(ar), jnp.sin(ac), jnp.cos(ac)], axis=-1)


def _hyena_pos_features(n):
    t_idx = jnp.arange(n, dtype=F32)
    t = jnp.linspace(0.0, 1.0, n, dtype=F32)
    bands = jnp.linspace(1e-4, HY_BANDS - 1, HY_BANDS, dtype=F32)
    ang = (2.0 * math.pi / n) * t_idx[:, None] * bands[None, :]
    z = jnp.concatenate([t[:, None], jnp.cos(ang), jnp.sin(ang)], axis=-1)
    return jnp.pad(z, ((0, 0), (0, LANE - z.shape[1])))


def _pad2(a, rows, cols):
    return jnp.pad(a, ((0, rows - a.shape[0]), (0, cols - a.shape[1])))


def kernel(x_prompt, x_sample, state_ret_fwd, state_ret_bwd, c, c_ctx, norm_g, mod_w, mod_b, ffn_a_in, ffn_a_out, ffn_b_in, ffn_b_out, ev_in_w, ev_out_w, fnet_w, hy_conv_w, hy_conv_b, hy_w1, hy_b1, hy_w2, hy_b2, hy_w_out, hy_freq, hy_decay, hy_bias, od_in_w, od_out_w, ret_log_decay_fwd, ret_log_decay_bwd, ret_gn, pool_w, pool_scale, final_norm):
    depth, d = norm_g.shape[0], norm_g.shape[2]
    half = d // 2
    dec_b = x_sample.shape[0]
    assert dec_b + 1 <= MOD_ROWS

    cond = jnp.concatenate([c, c_ctx[None, :]], axis=0)
    cond = jnp.pad(cond, ((0, MOD_ROWS - cond.shape[0]), (0, 0)))
    mod = _mod_call(cond, mod_w, mod_b).reshape(depth, MOD_ROWS, N_MOD, 1, d)

    bf = lambda a: a.astype(BF16)
    ffn_a_in, ffn_a_out, ffn_b_in, ffn_b_out = bf(ffn_a_in), bf(ffn_a_out), bf(ffn_b_in), bf(ffn_b_out)
    ev_in_w, ev_out_w, od_in_w, od_out_w = bf(ev_in_w), bf(ev_out_w), bf(od_in_w), bf(od_out_w)
    fnet_w, pool_w = bf(fnet_w), bf(pool_w)

    hid = hy_w1.shape[2]
    n_even = hy_w1.shape[0]
    hy_w1p = jnp.stack([_pad2(hy_w1[e], LANE, LANE) for e in range(n_even)])
    hy_w2p = jnp.stack([_pad2(hy_w2[e], LANE, LANE) for e in range(n_even)])
    hy_woutp = jnp.pad(hy_w_out, ((0, 0), (0, LANE - hid), (0, 0)))
    padv = lambda a: jnp.pad(a, ((0, 0), (0, LANE - hid)))[:, None, :]
    hy_b1p, hy_b2p, hy_freqp = padv(hy_b1), padv(hy_b2), padv(hy_freq)

    gc = half // N_GROUPS
    cs_chan = jnp.concatenate(_dft_tables(jnp.arange(gc), jnp.arange(gc), gc), axis=1)

    def trunk(x3, mod_rows, s0_f, s0_b, pos=None):
        b, n, _ = x3.shape
        t = b * n
        rows_per_mod = n if mod_rows.stop - mod_rows.start > 1 else t
        x = x3.reshape(t, d)
        hy_fwd_t, hy_inv_t = _hyena_tables(n)
        fnet_t = _parity_tables(n, n)
        z_pos = _hyena_pos_features(n)
        s_f_out, s_b_out = [], []
        for l in range(depth):
            m = [mod[l, mod_rows, i] for i in range(N_MOD)]
            x = _ffn_call(x, norm_g[l, 0][None], m[0], m[1], m[2], ffn_a_in, ffn_a_out, l, rows_per_mod,
                          pos=pos if l == 0 else None)
            if l % 2 == 0:
                e = l // 2
                u3 = _inproj_call(x, norm_g[l, 1][None], m[3], m[4], ev_in_w, e, rows_per_mod).reshape(b, n, -1)
                ya = _fnet_seq_call(fnet_t, *_fnet_chan_call(u3, cs_chan, half), fnet_w[e])
                yb = _hyena_mix(u3, half, half, hy_fwd_t, hy_inv_t, z_pos, hy_conv_w[e], hy_conv_b[e][None],
                                hy_w1p[e], hy_b1p[e], hy_w2p[e], hy_b2p[e], hy_freqp[e], hy_woutp[e],
                                hy_decay[e][None], hy_bias[e][None])
                w_out, w_idx = ev_out_w, e
            else:
                o = l // 2
                u3 = _inproj_call(x, norm_g[l, 1][None], m[3], m[4], od_in_w, o, rows_per_mod).reshape(b, n, -1)
                ya, s_f, s_b = _ret_call(u3, half, ret_log_decay_fwd[o], ret_log_decay_bwd[o], ret_gn[o][None],
                                         s0_f[:, o], s0_b[:, o])
                yb = _pool_call(u3, 4 * half, half, pool_w[o], pool_scale[o][None])
                s_f_out.append(s_f)
                s_b_out.append(s_b)
                w_out, w_idx = od_out_w, o
            x = _outproj_call(x, ya.reshape(t, half), yb.reshape(t, half), w_out, w_idx, m[5], rows_per_mod)
            x = _ffn_call(x, norm_g[l, 2][None], m[6], m[7], m[8], ffn_b_in, ffn_b_out, l, rows_per_mod,
                          final_g=final_norm[None] if l == depth - 1 else None)
        return x.reshape(b, n, d), jnp.stack(s_f_out, axis=1), jnp.stack(s_b_out, axis=1)

    zero_state = jnp.zeros((x_prompt.shape[0],) + state_ret_fwd.shape[1:], F32)
    y_prompt, st_f, st_b = trunk(x_prompt, slice(dec_b, dec_b + 1), zero_state, zero_state)

    y_sample, _, _ = trunk(x_sample, slice(0, dec_b), state_ret_fwd, state_ret_bwd,
                           pos=_grid_pos_embed(x_sample.shape[1], d))
    return (y_prompt, y_sample, st_f, st_b)
```
